```python
import jax, jax.numpy as jnp
from jax import lax
import numpy as np

D_MODEL = 1024
BATCH = 16
SEQ = 2048
DEPTH = 1

GRID_W = 64
CTX_LEN = 256
ATT_HEADS = 8
ATT_KV_HEADS = 2
ATT_GROUP = ATT_HEADS // ATT_KV_HEADS
ATT_HEAD_DIM = 64
ATT_WIDTH = ATT_HEADS * ATT_HEAD_DIM
ATT_KV_WIDTH = ATT_KV_HEADS * ATT_HEAD_DIM
WINDOW = 128
BLOCK = 128
ROPE_BASE = 10000.0
HG_WIDTH = D_MODEL // 2
HG_EXPAND = 128
HG_HEADS = HG_WIDTH // HG_EXPAND
CHUNK = 64
D_FF = 2816
N_MOD = 9
IN_WIDTH = ATT_WIDTH + 2 * ATT_KV_WIDTH + 5 * HG_WIDTH + 2 * D_MODEL
EPS = 1e-6
NEG_INF = -1e30

kernel_name = 'hybrid_dit_gqa_hgrn2_macaron'


def rmsnorm(x, g):
    x32 = x.astype(jnp.float32)
    y = x32 * lax.rsqrt(jnp.mean(x32 * x32, axis=-1, keepdims=True) + EPS)
    return (y * g.astype(jnp.float32)).astype(x.dtype)


def swiglu(h, w_in, w_out):
    gate, up = jnp.split(h @ w_in, 2, axis=-1)
    return (jax.nn.silu(gate) * up) @ w_out


def ada_pre(z, g, shift, scale):
    return rmsnorm(z, g) * (1 + scale) + shift


def ada_post(z, y, g, gate, w):
    return z + w * gate * rmsnorm(y, g)


def rope_1d(x, pos):
    half = x.shape[-1] // 2
    freqs = ROPE_BASE ** (-jnp.arange(half, dtype=jnp.float32) / half)
    ang = pos.astype(jnp.float32)[:, None] * freqs[None, :]
    cos, sin = jnp.cos(ang).astype(x.dtype), jnp.sin(ang).astype(x.dtype)
    x1, x2 = x[..., :half], x[..., half:]
    return jnp.concatenate([x1 * cos - x2 * sin, x1 * sin + x2 * cos], axis=-1)


def rope_2d(x, row, col):
    h = x.shape[-1] // 2
    return jnp.concatenate([rope_1d(x[..., :h], row), rope_1d(x[..., h:], col)], axis=-1)


def split_in(p):
    sizes = (ATT_WIDTH, ATT_KV_WIDTH, ATT_KV_WIDTH, HG_WIDTH, HG_WIDTH, HG_WIDTH, HG_WIDTH, HG_WIDTH, D_MODEL, D_MODEL)
    points = [int(v) for v in np.cumsum(sizes)[:-1]]
    return jnp.split(p, points, axis=-1)


def q_heads(z):
    B, T, _ = z.shape
    return z.reshape(B, T, ATT_KV_HEADS, ATT_GROUP, ATT_HEAD_DIM).transpose(0, 2, 3, 1, 4)


def kv_heads(z):
    B, T, _ = z.shape
    return z.reshape(B, T, ATT_KV_HEADS, ATT_HEAD_DIM).transpose(0, 2, 1, 3)


def hg_heads(z):
    B, T, _ = z.shape
    return z.reshape(B, T, HG_HEADS, HG_EXPAND).transpose(0, 2, 1, 3)


def hg_merge(o):
    B, H, T, d = o.shape
    return o.transpose(0, 2, 1, 3).reshape(B, T, H * d)


def softmax_with_sink(logits, sink):
    sink_col = jnp.broadcast_to(sink, logits.shape[:-1] + (1,))
    p = jax.nn.softmax(jnp.concatenate([logits, sink_col], axis=-1), axis=-1)
    return p[..., :-1]


def window_attention(q, k, v, kc, vc, sink):
    B, KVH, G, T, hd = q.shape
    nb = T // BLOCK
    scale = hd ** -0.5
    qb = q.reshape(B, KVH, G, nb, BLOCK, hd)

    def band(z):
        zp = jnp.pad(z, ((0, 0), (0, 0), (BLOCK, BLOCK), (0, 0))).reshape(B, KVH, nb + 2, BLOCK, hd)
        return jnp.concatenate([zp[:, :, :nb], zp[:, :, 1:nb + 1], zp[:, :, 2:]], axis=3)

    kb, vb = band(k), band(v)
    blk = jnp.arange(nb)[:, None, None] * BLOCK
    qpos = blk + jnp.arange(BLOCK)[None, :, None]
    kpos = blk - BLOCK + jnp.arange(3 * BLOCK)[None, None, :]
    valid = (jnp.abs(kpos - qpos) <= WINDOW) & (kpos >= 0) & (kpos < T)
    s_band = jnp.einsum('bkgnqd,bknsd->bkgnqs', qb, kb).astype(jnp.float32) * scale
    s_band = jnp.where(valid, s_band, NEG_INF)
    s_ctx = jnp.einsum('bkgnqd,bksd->bkgnqs', qb, kc).astype(jnp.float32) * scale
    sink_b = sink.astype(jnp.float32).reshape(KVH, G)[None, :, :, None, None, None]
    p = softmax_with_sink(jnp.concatenate([s_band, s_ctx], axis=-1), sink_b).astype(v.dtype)
    o = (jnp.einsum('bkgnqs,bknsd->bkgnqd', p[..., :3 * BLOCK], vb)
         + jnp.einsum('bkgnqs,bksd->bkgnqd', p[..., 3 * BLOCK:], vc))
    return o.reshape(B, KVH, G, T, hd).transpose(0, 3, 1, 2, 4).reshape(B, T, KVH * G * hd)


def context_attention(qc, kc, vc, sink):
    B, KVH, G, L, hd = qc.shape
    s = jnp.einsum('bkgqd,bksd->bkgqs', qc, kc).astype(jnp.float32) * (hd ** -0.5)
    sink_b = sink.astype(jnp.float32).reshape(KVH, G)[None, :, :, None, None]
    p = softmax_with_sink(s, sink_b).astype(vc.dtype)
    o = jnp.einsum('bkgqs,bksd->bkgqd', p, vc)
    return o.transpose(0, 3, 1, 2, 4).reshape(B, L, KVH * G * hd)


def gated_scan(q, k, v, logf, s0):
    B, H, T, dk = q.shape
    dv = v.shape[-1]
    n = T // CHUNK

    def chunks(z):
        return jnp.moveaxis(z.reshape(B, H, n, CHUNK, z.shape[-1]), 2, 0)

    tril = jnp.tril(jnp.ones((CHUNK, CHUNK), dtype=bool))

    def step(S, inp):
        qc, kc, vc, lf = inp
        b = jnp.cumsum(lf, axis=-2)
        o_inter = jnp.einsum('bhtd,bhde->bhte', qc * jnp.exp(b), S)
        diff = b[:, :, :, None, :] - b[:, :, None, :, :]
        decay = jnp.where(tril[:, :, None], jnp.exp(jnp.minimum(diff, 0.0)), 0.0)
        scores = jnp.einsum('bhtsd,bhsd->bhts', qc[:, :, :, None, :] * decay, kc)
        o_intra = jnp.einsum('bhts,bhse->bhte', scores, vc)
        b_last = b[:, :, -1:, :]
        k_dec = kc * jnp.exp(b_last - b)
        S_new = jnp.exp(b_last[:, :, 0, :])[..., None] * S + jnp.einsum('bhsd,bhse->bhde', k_dec, vc)
        return S_new, o_inter + o_intra

    S_fin, o = lax.scan(step, s0, (chunks(q), chunks(k), chunks(v), chunks(logf)))
    return jnp.moveaxis(o, 0, 2).reshape(B, H, T, dv), S_fin


def hgrn_bidir(q, v, lf_f, k_f, lf_b, k_b, s0_f, s0_b):
    flip = lambda z: jnp.flip(z, axis=2)
    o_f, s_f = gated_scan(q, k_f, v, lf_f, s0_f)
    o_b, s_b = gated_scan(flip(q), flip(k_b), flip(v), flip(lf_b), s0_b)
    return o_f + flip(o_b), s_f, s_b


def hg_forget(z, lb):
    f = lb + (1.0 - lb) * jax.nn.sigmoid(z.astype(jnp.float32))
    return hg_heads(jnp.log(f)), hg_heads(1.0 - f)


def mixer(h, hc, w_in, sink, lb_f, lb_b, g_hnorm, w_o_attn, w_o_hgrn, w_out, need_ctx):
    B, T, _ = h.shape
    rows = T // GRID_W
    row = jnp.repeat(jnp.arange(rows, dtype=jnp.int32), GRID_W)
    col = jnp.tile(jnp.arange(GRID_W, dtype=jnp.int32), rows)
    f32 = jnp.float32
    aq, ak, av, hq, hff, hfb, hi, hg, ga, gh = split_in(h @ w_in)
    aqc, akc, avc, hqc, hffc, hfbc, hic, hgc, gac, ghc = split_in(hc @ w_in)
    q = rope_2d(q_heads(aq), row, col)
    k = rope_2d(kv_heads(ak), row, col)
    kc, vc = kv_heads(akc), kv_heads(avc)
    o_att = window_attention(q, k, kv_heads(av), kc, vc, sink)
    qhc = hg_heads(jax.nn.silu(hqc.astype(f32)))
    vhc = hg_heads(hic.astype(f32))
    lfc_f, kc_f = hg_forget(hffc, lb_f)
    lfc_b, kc_b = hg_forget(hfbc, lb_b)
    s0 = jnp.zeros((B, HG_HEADS, HG_EXPAND, HG_EXPAND), f32)
    o_hc, s_f, s_b = hgrn_bidir(qhc, vhc, lfc_f, kc_f, lfc_b, kc_b, s0, s0)
    qh = hg_heads(jax.nn.silu(hq.astype(f32)))
    vh = hg_heads(hi.astype(f32))
    lf_f, k_f = hg_forget(hff, lb_f)
    lf_b, k_b = hg_forget(hfb, lb_b)
    o_h, _, _ = hgrn_bidir(qh, vh, lf_f, k_f, lf_b, k_b, s_f, s_b)

    def readout(o, gate):
        return rmsnorm(hg_merge(o), g_hnorm).astype(gate.dtype) * jax.nn.silu(gate)

    def merge(o_a, o_r, g_a, g_r):
        return (jax.nn.sigmoid(g_a) * (o_a @ w_o_attn) + jax.nn.sigmoid(g_r) * (o_r @ w_o_hgrn)) @ w_out

    y = merge(o_att, readout(o_h, hg), ga, gh)
    if not need_ctx:
        return y, None
    o_attc = context_attention(q_heads(aqc), kc, vc, sink)
    yc = merge(o_attc, readout(o_hc, hgc), gac, ghc)
    return y, yc


def setup_inputs(seed: int = 0) -> dict:
    key = jax.random.key(seed)
    ks = jax.random.split(key, 20)
    f32 = jnp.float32

    def nrm(k, shape, scale):
        return jax.random.normal(k, shape, f32) * scale

    D = D_MODEL
    return {
        'x': nrm(ks[0], (BATCH, SEQ, D), 1.0),
        'c': nrm(ks[1], (BATCH, D), 1.0),
        'ctx': nrm(ks[2], (BATCH, CTX_LEN, D), 1.0),
        'c_ctx': nrm(ks[3], (D,), 1.0),
        'w_ada': nrm(ks[4], (DEPTH, D, N_MOD * D), 0.5 * D ** -0.5),
        'b_ada': nrm(ks[5], (DEPTH, N_MOD * D), 0.02),
        'norm_pre': 1.0 + nrm(ks[6], (DEPTH, 3, D), 0.02),
        'norm_post': 1.0 + nrm(ks[7], (DEPTH, 3, D), 0.02),
        'ffn1_w_in': nrm(ks[8], (DEPTH, D, 2 * D_FF), D ** -0.5),
        'ffn1_w_out': nrm(ks[9], (DEPTH, D_FF, D), D_FF ** -0.5),
        'ffn2_w_in': nrm(ks[10], (DEPTH, D, 2 * D_FF), D ** -0.5),
        'ffn2_w_out': nrm(ks[11], (DEPTH, D_FF, D), D_FF ** -0.5),
        'mix_w_in': nrm(ks[12], (DEPTH, D, IN_WIDTH), D ** -0.5),
        'attn_sink': nrm(ks[13], (DEPTH, ATT_HEADS), 0.5),
        'hgrn_lb_fwd': nrm(ks[14], (DEPTH + 1, HG_WIDTH), 0.5),
        'hgrn_lb_bwd': nrm(ks[15], (DEPTH + 1, HG_WIDTH), 0.5),
        'hgrn_norm': 1.0 + nrm(ks[16], (DEPTH, HG_WIDTH), 0.02),
        'w_o_attn': nrm(ks[17], (DEPTH, ATT_WIDTH, D), ATT_WIDTH ** -0.5),
        'w_o_hgrn': nrm(ks[18], (DEPTH, HG_WIDTH, D), HG_WIDTH ** -0.5),
        'w_out': nrm(ks[19], (DEPTH, D, D), D ** -0.5),
    }


def reference(x, c, ctx, c_ctx, w_ada, b_ada, norm_pre, norm_post, ffn1_w_in, ffn1_w_out,
              ffn2_w_in, ffn2_w_out, mix_w_in, attn_sink, hgrn_lb_fwd, hgrn_lb_bwd, hgrn_norm,
              w_o_attn, w_o_hgrn, w_out):
    B, T, D = x.shape
    lb_f_all = jnp.cumsum(jax.nn.softmax(hgrn_lb_fwd.astype(jnp.float32), axis=0), axis=0)
    lb_b_all = jnp.cumsum(jax.nn.softmax(hgrn_lb_bwd.astype(jnp.float32), axis=0), axis=0)
    xc = ctx
    for l in range(DEPTH):
        need_ctx = l < DEPTH - 1
        mod = (jax.nn.silu(c) @ w_ada[l] + b_ada[l]).reshape(B, N_MOD, 1, D)
        mod_c = (jax.nn.silu(c_ctx) @ w_ada[l] + b_ada[l]).reshape(N_MOD, D)
        m = [mod[:, j] for j in range(N_MOD)]
        mc = [mod_c[j] for j in range(N_MOD)]
        h = ada_pre(x, norm_pre[l, 0], m[0], m[1])
        x = ada_post(x, swiglu(h, ffn1_w_in[l], ffn1_w_out[l]), norm_post[l, 0], m[2], 0.5)
        hc = ada_pre(xc, norm_pre[l, 0], mc[0], mc[1])
        xc = ada_post(xc, swiglu(hc, ffn1_w_in[l], ffn1_w_out[l]), norm_post[l, 0], mc[2], 0.5)
        h = ada_pre(x, norm_pre[l, 1], m[3], m[4])
        hc = ada_pre(xc, norm_pre[l, 1], mc[3], mc[4])
        y, yc = mixer(h, hc, mix_w_in[l], attn_sink[l], lb_f_all[l], lb_b_all[l], hgrn_norm[l],
                      w_o_attn[l], w_o_hgrn[l], w_out[l], need_ctx)
        x = ada_post(x, y, norm_post[l, 1], m[5], 1.0)
        h = ada_pre(x, norm_pre[l, 2], m[6], m[7])
        x = ada_post(x, swiglu(h, ffn2_w_in[l], ffn2_w_out[l]), norm_post[l, 2], m[8], 0.5)
        if need_ctx:
            xc = ada_post(xc, yc, norm_post[l, 1], mc[5], 1.0)
            hc = ada_pre(xc, norm_pre[l, 2], mc[6], mc[7])
            xc = ada_post(xc, swiglu(hc, ffn2_w_in[l], ffn2_w_out[l]), norm_post[l, 2], mc[8], 0.5)
    return x
```

```python
import functools

import jax
import jax.numpy as jnp
from jax import lax
from jax.experimental import pallas as pl
from jax.experimental.pallas import tpu as pltpu

F32 = jnp.float32
MXU_DTYPE = jnp.bfloat16

N_MOD = 9
GRID_W = 64
ATT_HEADS = 8
ATT_KV_HEADS = 2
ATT_GROUP = ATT_HEADS // ATT_KV_HEADS
ATT_HEAD_DIM = 64
ATT_WIDTH = ATT_HEADS * ATT_HEAD_DIM
ATT_KV_WIDTH = ATT_KV_HEADS * ATT_HEAD_DIM
WINDOW = 128
ATT_BLOCK = 128
ROPE_BASE = 10000.0
HG_EXPAND = 128
EPS = 1e-6
NEG_INF = -1e30

LANES = 128
HG_CHUNK = 64
HG_SUB = 16
VMEM_LIMIT = 56 * 1024 * 1024


def _sigmoid(x):
    return 1.0 / (1.0 + jnp.exp(-x))


def _silu(x):
    return x * _sigmoid(x)


def _rms(x, g):
    ms = jnp.mean(x * x, axis=-1, keepdims=True)
    return x * lax.rsqrt(ms + EPS) * g


def _mm(a, b):
    return jnp.dot(a.astype(MXU_DTYPE), b.astype(MXU_DTYPE), preferred_element_type=F32)


def _mm_nt(a, b):
    return lax.dot_general(a.astype(MXU_DTYPE), b.astype(MXU_DTYPE), (((1,), (1,)), ((), ())),
                           preferred_element_type=F32)


def _mm_tn(a, b):
    return lax.dot_general(a.astype(MXU_DTYPE), b.astype(MXU_DTYPE), (((0,), (0,)), ((), ())),
                           preferred_element_type=F32)


def _resident(shape):
    nd = len(shape)
    return pl.BlockSpec(shape, lambda *_: (0,) * nd, pipeline_mode=pl.Buffered(1))


def _params(n_axes):
    return pltpu.CompilerParams(dimension_semantics=("arbitrary",) * n_axes,
                                vmem_limit_bytes=VMEM_LIMIT)


def _mod_kernel(c_ref, w_ref, b_ref, o_ref):
    o_ref[...] = _mm(_silu(c_ref[...]), w_ref[...]) + b_ref[...]


def _modulation(cs, w_ada, b_ada):
    rows, d = cs.shape
    n = w_ada.shape[1]
    tn = n // 8
    return pl.pallas_call(
        _mod_kernel,
        grid=(n // tn,),
        in_specs=[pl.BlockSpec((rows, d), lambda j: (0, 0)),
                  pl.BlockSpec((d, tn), lambda j: (0, j)),
                  pl.BlockSpec((1, tn), lambda j: (0, j))],
        out_specs=pl.BlockSpec((rows, tn), lambda j: (0, j)),
        out_shape=jax.ShapeDtypeStruct((rows, n), F32),
        compiler_params=_params(1),
        name="modulation",
    )(cs, w_ada, b_ada.reshape(1, n))


def _ffn_kernel(x_ref, mod_ref, gpre_ref, gpost_ref, win_ref, wout_ref, o_ref, *, j0, d_ff):
    x = x_ref[...]
    shift, scale, gate = mod_ref[j0:j0 + 1, :], mod_ref[j0 + 1:j0 + 2, :], mod_ref[j0 + 2:j0 + 3, :]
    h = _rms(x, gpre_ref[...]) * (1.0 + scale) + shift
    gu = _mm(h, win_ref[...])
    act = _silu(gu[:, :d_ff]) * gu[:, d_ff:]
    y = _mm(act, wout_ref[...])
    o_ref[...] = x + 0.5 * gate * _rms(y, gpost_ref[...])


def _ffn(x, mod3, mod_row, j0, g_pre, g_post, w_in, w_out, tm):
    rows, d = x.shape
    d_ff = w_out.shape[0]
    return pl.pallas_call(
        functools.partial(_ffn_kernel, j0=j0, d_ff=d_ff),
        grid=(rows // tm,),
        in_specs=[pl.BlockSpec((tm, d), lambda i: (i, 0)),
                  pl.BlockSpec((None, N_MOD, d), lambda i: (mod_row(i), 0, 0)),
                  _resident((1, d)), _resident((1, d)),
                  _resident(w_in.shape), _resident(w_out.shape)],
        out_specs=pl.BlockSpec((tm, d), lambda i: (i, 0)),
        out_shape=jax.ShapeDtypeStruct((rows, d), F32),
        compiler_params=_params(1),
        name="ffn",
    )(x, mod3, g_pre, g_post, w_in, w_out)


def _lower_bound(lb_ref):
    a = lb_ref[...]
    e = jnp.exp(a - jnp.max(a, axis=0, keepdims=True))
    return e[0:1, :] / jnp.sum(e, axis=0, keepdims=True)


def _forget(z, lb):
    f = lb + (1.0 - lb) * _sigmoid(z)
    return jnp.log(f), 1.0 - f


def _rope(x, cos, sin):
    w = x.shape[1]
    reps = w // LANES
    cos_w = jnp.concatenate([cos] * reps, axis=1) if reps > 1 else cos
    sin_w = jnp.concatenate([sin] * reps, axis=1) if reps > 1 else sin
    lane = lax.broadcasted_iota(jnp.int32, x.shape, 1)
    quarter = ATT_HEAD_DIM // 4
    first = (lane % (2 * quarter)) < quarter
    partner = jnp.where(first, pltpu.roll(x, w - quarter, 1), pltpu.roll(x, quarter, 1))
    return x * cos_w + partner * sin_w


def _inproj_kernel(x_ref, mod_ref, gpre_ref, cos_ref, sin_ref, lbf_ref, lbb_ref, w_ref,
                   q_ref, kv_ref, hq_ref, lff_ref, kff_ref, lfb_ref, kfb_ref, hv_ref, og_ref,
                   sga_ref, sgh_ref, *, d, hw):
    x = x_ref[...]
    shift, scale = mod_ref[3:4, :], mod_ref[4:5, :]
    h = (_rms(x, gpre_ref[...]) * (1.0 + scale) + shift).astype(MXU_DTYPE)
    cos, sin = cos_ref[...], sin_ref[...]
    c0 = 0

    def proj(width):
        nonlocal c0
        out = jnp.dot(h, w_ref[:, c0:c0 + width], preferred_element_type=F32)
        c0 += width
        return out

    q = proj(ATT_WIDTH)
    q_ref[...] = (_rope(q, cos, sin) * (ATT_HEAD_DIM ** -0.5)).astype(q_ref.dtype)
    kv = proj(2 * ATT_KV_WIDTH)
    kv_ref[:, :ATT_KV_WIDTH] = _rope(kv[:, :ATT_KV_WIDTH], cos, sin).astype(kv_ref.dtype)
    kv_ref[:, ATT_KV_WIDTH:] = kv[:, ATT_KV_WIDTH:].astype(kv_ref.dtype)
    hq_ref[...] = _silu(proj(hw))
    lf, kf = _forget(proj(hw), _lower_bound(lbf_ref))
    lff_ref[...] = lf
    kff_ref[...] = kf
    lf, kf = _forget(proj(hw), _lower_bound(lbb_ref))
    lfb_ref[...] = lf
    kfb_ref[...] = kf
    hv_ref[...] = proj(hw)
    og_ref[...] = _silu(proj(hw))
    sga_ref[...] = _sigmoid(proj(d))
    sgh_ref[...] = _sigmoid(proj(d))


def _inproj(x, mod3, tiles_per_batch, g_pre, cos, sin, lb_f, lb_b, w, tm):
    rows, d = x.shape
    hw = lb_f.shape[1]
    row = lambda width: pl.BlockSpec((tm, width), lambda i: (i, 0))
    widths = [ATT_WIDTH, 2 * ATT_KV_WIDTH, hw, hw, hw, hw, hw, hw, hw, d, d]
    dtypes = [MXU_DTYPE, MXU_DTYPE] + [F32] * 9
    return pl.pallas_call(
        functools.partial(_inproj_kernel, d=d, hw=hw),
        grid=(rows // tm,),
        in_specs=[row(d),
                  pl.BlockSpec((None, N_MOD, d), lambda i: (i // tiles_per_batch, 0, 0)),
                  _resident((1, d)),
                  pl.BlockSpec((tm, LANES), lambda i: (i % tiles_per_batch, 0)),
                  pl.BlockSpec((tm, LANES), lambda i: (i % tiles_per_batch, 0)),
                  _resident(lb_f.shape), _resident(lb_b.shape), _resident(w.shape)],
        out_specs=[row(wd) for wd in widths],
        out_shape=[jax.ShapeDtypeStruct((rows, wd), dt) for wd, dt in zip(widths, dtypes)],
        compiler_params=_params(1),
        name="inproj",
    )(x, mod3, g_pre, cos, sin, lb_f, lb_b, w)


def _inproj_ctx_kernel(x_ref, mod_ref, gpre_ref, lbf_ref, lbb_ref, w_ref,
                       kv_ref, lff_ref, kff_ref, lfb_ref, kfb_ref, hv_ref, *, hw):
    x = x_ref[...]
    shift, scale = mod_ref[3:4, :], mod_ref[4:5, :]
    h = (_rms(x, gpre_ref[...]) * (1.0 + scale) + shift).astype(MXU_DTYPE)
    kvw = 2 * ATT_KV_WIDTH
    kv_ref[...] = jnp.dot(h, w_ref[:, :kvw], preferred_element_type=F32).astype(kv_ref.dtype)
    lf, kf = _forget(jnp.dot(h, w_ref[:, kvw:kvw + hw], preferred_element_type=F32), _lower_bound(lbf_ref))
    lff_ref[...] = lf
    kff_ref[...] = kf
    lf, kf = _forget(jnp.dot(h, w_ref[:, kvw + hw:kvw + 2 * hw], preferred_element_type=F32),
                     _lower_bound(lbb_ref))
    lfb_ref[...] = lf
    kfb_ref[...] = kf
    hv_ref[...] = jnp.dot(h, w_ref[:, kvw + 2 * hw:], preferred_element_type=F32)


def _inproj_ctx(xc, mod3, ctx_row, g_pre, lb_f, lb_b, w, tm):
    rows, d = xc.shape
    hw = lb_f.shape[1]
    row = lambda width: pl.BlockSpec((tm, width), lambda i: (i, 0))
    widths = [2 * ATT_KV_WIDTH, hw, hw, hw, hw, hw]
    dtypes = [MXU_DTYPE] + [F32] * 5
    return pl.pallas_call(
        functools.partial(_inproj_ctx_kernel, hw=hw),
        grid=(rows // tm,),
        in_specs=[row(d),
                  pl.BlockSpec((None, N_MOD, d), lambda i: (ctx_row, 0, 0)),
                  _resident((1, d)), _resident(lb_f.shape), _resident(lb_b.shape), _resident(w.shape)],
        out_specs=[row(wd) for wd in widths],
        out_shape=[jax.ShapeDtypeStruct((rows, wd), dt) for wd, dt in zip(widths, dtypes)],
        compiler_params=_params(1),
        name="inproj_ctx",
    )(xc, mod3, g_pre, lb_f, lb_b, w)


def _attn_kernel(sink_ref, q_ref, kv_ref, kvc_ref, o_ref, *, seq, band):
    n = pl.program_id(1)
    start = pl.multiple_of(jnp.clip(n * ATT_BLOCK - ATT_BLOCK, 0, seq - band), ATT_BLOCK)
    kvb = kv_ref[pl.ds(start, band), :]
    kvc = kvc_ref[...]
    k_all = jnp.concatenate([kvb[:, :ATT_KV_WIDTH], kvc[:, :ATT_KV_WIDTH]], axis=0)
    v_all = jnp.concatenate([kvb[:, ATT_KV_WIDTH:], kvc[:, ATT_KV_WIDTH:]], axis=0)
    n_keys = k_all.shape[0]
    qpos = n * ATT_BLOCK + lax.broadcasted_iota(jnp.int32, (ATT_BLOCK, n_keys), 0)
    col = lax.broadcasted_iota(jnp.int32, (ATT_BLOCK, n_keys), 1)
    valid = (jnp.abs(start + col - qpos) <= WINDOW) | (col >= band)
    low = lax.broadcasted_iota(jnp.int32, (ATT_BLOCK, LANES), 1) < ATT_HEAD_DIM

    def head(qh, sink):
        s = jnp.where(valid, _mm_nt(qh, k_all), NEG_INF)
        m = jnp.maximum(jnp.max(s, axis=-1, keepdims=True), sink)
        e = jnp.exp(s - m)
        den = jnp.sum(e, axis=-1, keepdims=True) + jnp.exp(sink - m)
        return _mm(e / den, v_all)

    for m in range(ATT_GROUP):
        qm = q_ref[:, m * LANES:(m + 1) * LANES]
        zero = jnp.zeros_like(qm)
        o_lo = head(jnp.where(low, qm, zero), sink_ref[m])
        o_hi = head(jnp.where(low, zero, qm), sink_ref[ATT_GROUP + m])
        o_ref[:, m * LANES:(m + 1) * LANES] = jnp.where(low, o_lo, o_hi).astype(o_ref.dtype)


def _attention(sink, q, kv, kvc, batch):
    rows, _ = q.shape
    seq = rows // batch
    ctx_len = kvc.shape[0] // batch
    nb = seq // ATT_BLOCK
    band = 3 * ATT_BLOCK
    return pl.pallas_call(
        functools.partial(_attn_kernel, seq=seq, band=band),
        grid=(batch, nb),
        in_specs=[pl.BlockSpec(memory_space=pltpu.SMEM),
                  pl.BlockSpec((ATT_BLOCK, ATT_WIDTH), lambda b, n: (b * nb + n, 0)),
                  pl.BlockSpec((seq, 2 * ATT_KV_WIDTH), lambda b, n: (b, 0)),
                  pl.BlockSpec((ctx_len, 2 * ATT_KV_WIDTH), lambda b, n: (b, 0))],
        out_specs=pl.BlockSpec((ATT_BLOCK, ATT_WIDTH), lambda b, n: (b * nb + n, 0)),
        out_shape=jax.ShapeDtypeStruct((rows, ATT_WIDTH), MXU_DTYPE),
        compiler_params=_params(2),
        name="attention",
    )(sink, q, kv, kvc)


def _cum_decay(lf, tri3):
    hi = lf.astype(jnp.bfloat16)
    r1 = lf - hi.astype(F32)
    mid = r1.astype(jnp.bfloat16)
    lo = (r1 - mid.astype(F32)).astype(jnp.bfloat16)
    return jnp.dot(tri3, jnp.concatenate([hi, mid, lo], axis=0), preferred_element_type=F32)


def _hg_state_update(st, k, v, b, b_end):
    k_dec = k * jnp.exp(b_end - b)
    return st * jnp.exp(b_end) + _mm_tn(v, k_dec)


def _hg_chunk(q, k, v, lf, st, tri3, dmask, ones, rev):
    c, sub = HG_CHUNK, HG_SUB
    b = _cum_decay(lf, tri3)
    end = 0 if rev else c - 1
    b_end = b[end:end + 1, :]
    o_inter = _mm_nt(q * jnp.exp(b), st)
    blocks = []
    for i in range(c // sub):
        r0 = i * sub
        qi, bi = q[r0:r0 + sub], b[r0:r0 + sub]
        zs = [jnp.exp(jnp.minimum(bi - b[r0 + s:r0 + s + 1], 0.0)) * (qi * k[r0 + s:r0 + s + 1])
              for s in range(sub)]
        z = jnp.concatenate(zs, axis=0) * dmask
        scores = _mm(z, ones)
        o_i = scores[0:sub] * v[r0:r0 + 1]
        for s in range(1, sub):
            o_i = o_i + scores[s * sub:(s + 1) * sub] * v[r0 + s:r0 + s + 1]
        lo, hi = (r0 + sub, c) if rev else (0, r0)
        if hi > lo:
            edge = lo if rev else hi - 1
            b_edge = b[edge:edge + 1]
            q_t = qi * jnp.exp(bi - b_edge)
            k_t = k[lo:hi] * jnp.exp(b_edge - b[lo:hi])
            o_i = o_i + _mm(_mm_nt(q_t, k_t), v[lo:hi])
        blocks.append(o_i)
    o = o_inter + jnp.concatenate(blocks, axis=0)
    return o, _hg_state_update(st, k, v, b, b_end)


def _hgrn_kernel(q_ref, v_ref, lff_ref, kff_ref, lfb_ref, kfb_ref,
                 vc_ref, lffc_ref, kffc_ref, lfbc_ref, kfbc_ref, o_ref, sf_ref, sb_ref):
    c, sub = HG_CHUNK, HG_SUB
    seq, ctx_len = q_ref.shape[0], vc_ref.shape[0]
    n, nc = seq // c, ctx_len // c
    r = lax.broadcasted_iota(jnp.int32, (c, c), 0)
    s = lax.broadcasted_iota(jnp.int32, (c, c), 1)
    tril = (s <= r).astype(jnp.bfloat16)
    triu = (s >= r).astype(jnp.bfloat16)
    tri3_f = jnp.concatenate([tril] * 3, axis=1)
    tri3_b = jnp.concatenate([triu] * 3, axis=1)
    row = lax.broadcasted_iota(jnp.int32, (sub * sub, LANES), 0)
    dmask_f = ((row % sub) >= (row // sub)).astype(F32)
    dmask_b = ((row % sub) <= (row // sub)).astype(F32)
    ones = jnp.ones((LANES, LANES), MXU_DTYPE)

    sf_ref[...] = jnp.zeros_like(sf_ref)
    sb_ref[...] = jnp.zeros_like(sb_ref)

    def ctx_step(i, carry):
        rf = pl.multiple_of(i * c, c)
        rb = pl.multiple_of((nc - 1 - i) * c, c)
        bf = _cum_decay(lffc_ref[pl.ds(rf, c), :], tri3_f)
        sf_ref[...] = _hg_state_update(sf_ref[...], kffc_ref[pl.ds(rf, c), :], vc_ref[pl.ds(rf, c), :],
                                       bf, bf[c - 1:c, :])
        bb = _cum_decay(lfbc_ref[pl.ds(rb, c), :], tri3_b)
        sb_ref[...] = _hg_state_update(sb_ref[...], kfbc_ref[pl.ds(rb, c), :], vc_ref[pl.ds(rb, c), :],
                                       bb, bb[0:1, :])
        return carry

    lax.fori_loop(0, nc, ctx_step, 0)

    def step(i, carry, accumulate):
        rf = pl.multiple_of(i * c, c)
        rb = pl.multiple_of((n - 1 - i) * c, c)
        of, sf = _hg_chunk(q_ref[pl.ds(rf, c), :], kff_ref[pl.ds(rf, c), :], v_ref[pl.ds(rf, c), :],
                           lff_ref[pl.ds(rf, c), :], sf_ref[...], tri3_f, dmask_f, ones, rev=False)
        sf_ref[...] = sf
        ob, sb = _hg_chunk(q_ref[pl.ds(rb, c), :], kfb_ref[pl.ds(rb, c), :], v_ref[pl.ds(rb, c), :],
                           lfb_ref[pl.ds(rb, c), :], sb_ref[...], tri3_b, dmask_b, ones, rev=True)
        sb_ref[...] = sb
        if accumulate:
            o_ref[pl.ds(rf, c), :] += of
            o_ref[pl.ds(rb, c), :] += ob
        else:
            o_ref[pl.ds(rf, c), :] = of
            o_ref[pl.ds(rb, c), :] = ob
        return carry

    lax.fori_loop(0, n // 2, functools.partial(step, accumulate=False), 0)
    lax.fori_loop(n // 2, n, functools.partial(step, accumulate=True), 0)


def _hgrn(hq, hv, lff, kff, lfb, kfb, hvc, lffc, kffc, lfbc, kfbc, batch):
    rows, hw = hq.shape
    seq = rows // batch
    ctx_len = hvc.shape[0] // batch
    heads = hw // HG_EXPAND
    lat = pl.BlockSpec((seq, HG_EXPAND), lambda b, h: (b, h))
    ctx = pl.BlockSpec((ctx_len, HG_EXPAND), lambda b, h: (b, h))
    return pl.pallas_call(
        _hgrn_kernel,
        grid=(batch, heads),
        in_specs=[lat] * 6 + [ctx] * 5,
        out_specs=lat,
        out_shape=jax.ShapeDtypeStruct((rows, hw), F32),
        scratch_shapes=[pltpu.VMEM((HG_EXPAND, HG_EXPAND), F32)] * 2,
        compiler_params=_params(2),
        name="hgrn",
    )(hq, hv, lff, kff, lfb, kfb, hvc, lffc, kffc, lfbc, kfbc)


def _merge_kernel(x_ref, mod_ref, gpost_ref, ghn_ref, oa_ref, oh_ref, og_ref, sga_ref, sgh_ref,
                  woa_ref, woh_ref, wout_ref, o_ref):
    gate = mod_ref[5:6, :]
    r = _rms(oh_ref[...], ghn_ref[...]) * og_ref[...]
    z = sga_ref[...] * _mm(oa_ref[...], woa_ref[...]) + sgh_ref[...] * _mm(r, woh_ref[...])
    y = _mm(z, wout_ref[...])
    o_ref[...] = x_ref[...] + gate * _rms(y, gpost_ref[...])


def _merge(x, mod3, tiles_per_batch, g_post, g_hnorm, oa, oh, og, sga, sgh, woa, woh, wout, tm):
    rows, d = x.shape
    row = lambda width: pl.BlockSpec((tm, width), lambda i: (i, 0))
    return pl.pallas_call(
        _merge_kernel,
        grid=(rows // tm,),
        in_specs=[row(d),
                  pl.BlockSpec((None, N_MOD, d), lambda i: (i // tiles_per_batch, 0, 0)),
                  _resident((1, d)), _resident(g_hnorm.shape),
                  row(oa.shape[1]), row(oh.shape[1]), row(og.shape[1]), row(d), row(d),
                  _resident(woa.shape), _resident(woh.shape), _resident(wout.shape)],
        out_specs=row(d),
        out_shape=jax.ShapeDtypeStruct((rows, d), F32),
        compiler_params=_params(1),
        name="merge",
    )(x, mod3, g_post, g_hnorm, oa, oh, og, sga, sgh, woa, woh, wout)


def _rope_tables(seq):
    quarter = ATT_HEAD_DIM // 4
    t = jnp.arange(seq, dtype=jnp.int32)
    freqs = ROPE_BASE ** (-jnp.arange(quarter, dtype=F32) / quarter)
    cos, sin = [], []
    for pos in (t // GRID_W, t % GRID_W):
        ang = pos.astype(F32)[:, None] * freqs[None, :]
        cos += [jnp.cos(ang), jnp.cos(ang)]
        sin += [-jnp.sin(ang), jnp.sin(ang)]
    reps = LANES // ATT_HEAD_DIM
    return jnp.concatenate(cos * reps, axis=1), jnp.concatenate(sin * reps, axis=1)


def _pair_heads(n_rows_or_cols):
    order = []
    for m in range(ATT_GROUP):
        for h in (m, ATT_GROUP + m):
            order += list(range(h * ATT_HEAD_DIM, (h + 1) * ATT_HEAD_DIM))
    assert len(order) == n_rows_or_cols
    return jnp.asarray(order, dtype=jnp.int32)


def kernel(x, c, ctx, c_ctx, w_ada, b_ada, norm_pre, norm_post, ffn1_w_in, ffn1_w_out, ffn2_w_in, ffn2_w_out,
           mix_w_in, attn_sink, hgrn_lb_fwd, hgrn_lb_bwd, hgrn_norm, w_o_attn, w_o_hgrn, w_out):
    batch, seq, d = x.shape
    ctx_len = ctx.shape[1]
    depth = w_ada.shape[0]
    assert depth == 1, "single-layer problem: the context stream is only read, never written back"
    hw = hgrn_norm.shape[1]
    tm = 256
    tiles_per_batch = seq // tm
    bf = lambda a: a.astype(MXU_DTYPE)

    mod_rows = -(-(batch + 1) // 8) * 8
    cs = jnp.concatenate([c, c_ctx[None, :], jnp.zeros((mod_rows - batch - 1, d), F32)], axis=0)
    mod3 = _modulation(cs, w_ada[0], b_ada[0]).reshape(mod_rows, N_MOD, d)
    ctx_row = batch

    xl = x.reshape(batch * seq, d)
    xc = ctx.reshape(batch * ctx_len, d)
    pre = [norm_pre[0, j][None, :] for j in range(3)]
    post = [norm_post[0, j][None, :] for j in range(3)]

    w1_in, w1_out = bf(ffn1_w_in[0]), bf(ffn1_w_out[0])
    xl = _ffn(xl, mod3, lambda i: i // tiles_per_batch, 0, pre[0], post[0], w1_in, w1_out, tm)
    xc = _ffn(xc, mod3, lambda i: ctx_row, 0, pre[0], post[0], w1_in, w1_out, tm)

    wm = mix_w_in[0]
    w_lat = bf(jnp.concatenate([jnp.take(wm[:, :ATT_WIDTH], _pair_heads(ATT_WIDTH), axis=1), wm[:, ATT_WIDTH:]],
                               axis=1))
    kv0, hg0 = ATT_WIDTH, ATT_WIDTH + 2 * ATT_KV_WIDTH + hw
    w_ctx = bf(jnp.concatenate([wm[:, kv0:kv0 + 2 * ATT_KV_WIDTH], wm[:, hg0:hg0 + 3 * hw]], axis=1))
    cos, sin = _rope_tables(seq)
    q, kv, hq, lff, kff, lfb, kfb, hv, og, sga, sgh = _inproj(
        xl, mod3, tiles_per_batch, pre[1], cos, sin, hgrn_lb_fwd, hgrn_lb_bwd, w_lat, tm)
    kvc, lffc, kffc, lfbc, kfbc, hvc = _inproj_ctx(xc, mod3, ctx_row, pre[1], hgrn_lb_fwd, hgrn_lb_bwd, w_ctx, tm)

    o_att = _attention(attn_sink[0], q, kv, kvc, batch)
    o_hg = _hgrn(hq, hv, lff, kff, lfb, kfb, hvc, lffc, kffc, lfbc, kfbc, batch)

    woa = bf(jnp.take(w_o_attn[0], _pair_heads(ATT_WIDTH), axis=0))
    xl = _merge(xl, mod3, tiles_per_batch, post[1], hgrn_norm, o_att, o_hg, og, sga, sgh,
                woa, bf(w_o_hgrn[0]), bf(w_out[0]), tm)

    xl = _ffn(xl, mod3, lambda i: i // tiles_per_batch, 6, pre[2], post[2], bf(ffn2_w_in[0]), bf(ffn2_w_out[0]), tm)
    return xl.reshape(batch, seq, d)
```

```python
import functools

import jax
import jax.numpy as jnp
from jax import lax
from jax.experimental import pallas as pl
from jax.experimental.pallas import tpu as pltpu

F32 = jnp.float32
MXU_DTYPE = jnp.bfloat16

N_MOD = 9
GRID_W = 64
ATT_HEADS = 8
ATT_KV_HEADS = 2
ATT_GROUP = ATT_HEADS // ATT_KV_HEADS
ATT_HEAD_DIM = 64
ATT_WIDTH = ATT_HEADS * ATT_HEAD_DIM
ATT_KV_WIDTH = ATT_KV_HEADS * ATT_HEAD_DIM
WINDOW = 128
ATT_BLOCK = 128
ATT_SLABS = 1
ATT_QBLOCKS = 2
ROPE_BASE = 10000.0
HG_EXPAND = 128
EPS = 1e-6
NEG_INF = -1e30
LOG2E = 1.4426950408889634

LANES = 128
HG_CHUNK = 64
HG_SUB = 16
HG_UNROLL = 2
VMEM_LIMIT = 56 * 1024 * 1024


def _sigmoid(x):
    return 1.0 / (1.0 + jnp.exp(-x))


def _silu(x):
    return x * _sigmoid(x)


def _rms(x, g):
    ms = jnp.mean(x * x, axis=-1, keepdims=True)
    return x * lax.rsqrt(ms + EPS) * g


def _mm(a, b):
    return jnp.dot(a.astype(MXU_DTYPE), b.astype(MXU_DTYPE), preferred_element_type=F32)


def _mm_nt(a, b):
    return lax.dot_general(a.astype(MXU_DTYPE), b.astype(MXU_DTYPE), (((1,), (1,)), ((), ())),
                           preferred_element_type=F32)


def _mm_tn(a, b):
    return lax.dot_general(a.astype(MXU_DTYPE), b.astype(MXU_DTYPE), (((0,), (0,)), ((), ())),
                           preferred_element_type=F32)


def _resident(shape):
    nd = len(shape)
    return pl.BlockSpec(shape, lambda *_: (0,) * nd, pipeline_mode=pl.Buffered(1))


def _params(n_axes):
    return pltpu.CompilerParams(dimension_semantics=("arbitrary",) * n_axes,
                                vmem_limit_bytes=VMEM_LIMIT)


def _mod_kernel(c_ref, w_ref, b_ref, o_ref):
    o_ref[...] = _mm(_silu(c_ref[...]), w_ref[...]) + b_ref[...]


def _modulation(cs, w_ada, b_ada):
    rows, d = cs.shape
    n = w_ada.shape[1]
    tn = n // 8
    return pl.pallas_call(
        _mod_kernel,
        grid=(n // tn,),
        in_specs=[pl.BlockSpec((rows, d), lambda j: (0, 0)),
                  pl.BlockSpec((d, tn), lambda j: (0, j)),
                  pl.BlockSpec((1, tn), lambda j: (0, j))],
        out_specs=pl.BlockSpec((rows, tn), lambda j: (0, j)),
        out_shape=jax.ShapeDtypeStruct((rows, n), F32),
        compiler_params=_params(1),
        name="modulation",
    )(cs, w_ada, b_ada.reshape(1, n))


def _ffn_kernel(x_ref, mod_ref, gpre_ref, gpost_ref, win_ref, wout_ref, o_ref, *, j0, d_ff):
    x = x_ref[...]
    shift, scale, gate = mod_ref[j0:j0 + 1, :], mod_ref[j0 + 1:j0 + 2, :], mod_ref[j0 + 2:j0 + 3, :]
    h = _rms(x, gpre_ref[...]) * (1.0 + scale) + shift
    gu = _mm(h, win_ref[...])
    act = _silu(gu[:, :d_ff]) * gu[:, d_ff:]
    y = _mm(act, wout_ref[...])
    o_ref[...] = x + 0.5 * gate * _rms(y, gpost_ref[...])


def _ffn(x, mod3, mod_row, j0, g_pre, g_post, w_in, w_out, tm):
    rows, d = x.shape
    d_ff = w_out.shape[0]
    return pl.pallas_call(
        functools.partial(_ffn_kernel, j0=j0, d_ff=d_ff),
        grid=(rows // tm,),
        in_specs=[pl.BlockSpec((tm, d), lambda i: (i, 0)),
                  pl.BlockSpec((None, N_MOD, d), lambda i: (mod_row(i), 0, 0)),
                  _resident((1, d)), _resident((1, d)),
                  _resident(w_in.shape), _resident(w_out.shape)],
        out_specs=pl.BlockSpec((tm, d), lambda i: (i, 0)),
        out_shape=jax.ShapeDtypeStruct((rows, d), F32),
        compiler_params=_params(1),
        name="ffn",
    )(x, mod3, g_pre, g_post, w_in, w_out)


def _lower_bound(lb_ref):
    a = lb_ref[...]
    e = jnp.exp(a - jnp.max(a, axis=0, keepdims=True))
    return e[0:1, :] / jnp.sum(e, axis=0, keepdims=True)


def _forget(z, lb):
    f = lb + (1.0 - lb) * _sigmoid(z)
    return jnp.log2(f), 1.0 - f


def _rope(x, cos, sin):
    w = x.shape[1]
    reps = w // LANES
    cos_w = jnp.concatenate([cos] * reps, axis=1) if reps > 1 else cos
    sin_w = jnp.concatenate([sin] * reps, axis=1) if reps > 1 else sin
    lane = lax.broadcasted_iota(jnp.int32, x.shape, 1)
    quarter = ATT_HEAD_DIM // 4
    first = (lane % (2 * quarter)) < quarter
    partner = jnp.where(first, pltpu.roll(x, w - quarter, 1), pltpu.roll(x, quarter, 1))
    return x * cos_w + partner * sin_w


def _inproj_kernel(x_ref, mod_ref, gpre_ref, cos_ref, sin_ref, lbf_ref, lbb_ref, w_ref,
                   q_ref, kv_ref, hq_ref, lff_ref, kff_ref, lfb_ref, kfb_ref, hv_ref, og_ref,
                   sga_ref, sgh_ref, *, d, hw):
    x = x_ref[...]
    shift, scale = mod_ref[3:4, :], mod_ref[4:5, :]
    h = (_rms(x, gpre_ref[...]) * (1.0 + scale) + shift).astype(MXU_DTYPE)
    cos, sin = cos_ref[...], sin_ref[...]
    c0 = 0

    def proj(width):
        nonlocal c0
        out = jnp.dot(h, w_ref[:, c0:c0 + width], preferred_element_type=F32)
        c0 += width
        return out

    q = proj(ATT_WIDTH)
    q_ref[...] = (_rope(q, cos, sin) * (LOG2E * ATT_HEAD_DIM ** -0.5)).astype(q_ref.dtype)
    kv = proj(2 * ATT_KV_WIDTH)
    kv_ref[:, :ATT_KV_WIDTH] = _rope(kv[:, :ATT_KV_WIDTH], cos, sin).astype(kv_ref.dtype)
    kv_ref[:, ATT_KV_WIDTH:] = kv[:, ATT_KV_WIDTH:].astype(kv_ref.dtype)
    hq_ref[...] = _silu(proj(hw))
    lf, kf = _forget(proj(hw), _lower_bound(lbf_ref))
    lff_ref[...] = lf
    kff_ref[...] = kf
    lf, kf = _forget(proj(hw), _lower_bound(lbb_ref))
    lfb_ref[...] = lf
    kfb_ref[...] = kf
    hv_ref[...] = proj(hw)
    og_ref[...] = _silu(proj(hw))
    sga_ref[...] = _sigmoid(proj(d))
    sgh_ref[...] = _sigmoid(proj(d))


def _inproj(x, mod3, tiles_per_batch, g_pre, cos, sin, lb_f, lb_b, w, tm):
    rows, d = x.shape
    hw = lb_f.shape[1]
    row = lambda width: pl.BlockSpec((tm, width), lambda i: (i, 0))
    widths = [ATT_WIDTH, 2 * ATT_KV_WIDTH, hw, hw, hw, hw, hw, hw, hw, d, d]
    dtypes = [MXU_DTYPE, MXU_DTYPE] + [F32] * 9
    return pl.pallas_call(
        functools.partial(_inproj_kernel, d=d, hw=hw),
        grid=(rows // tm,),
        in_specs=[row(d),
                  pl.BlockSpec((None, N_MOD, d), lambda i: (i // tiles_per_batch, 0, 0)),
                  _resident((1, d)),
                  pl.BlockSpec((tm, LANES), lambda i: (i % tiles_per_batch, 0)),
                  pl.BlockSpec((tm, LANES), lambda i: (i % tiles_per_batch, 0)),
                  _resident(lb_f.shape), _resident(lb_b.shape), _resident(w.shape)],
        out_specs=[row(wd) for wd in widths],
        out_shape=[jax.ShapeDtypeStruct((rows, wd), dt) for wd, dt in zip(widths, dtypes)],
        compiler_params=_params(1),
        name="inproj",
    )(x, mod3, g_pre, cos, sin, lb_f, lb_b, w)


def _inproj_ctx_kernel(x_ref, mod_ref, gpre_ref, lbf_ref, lbb_ref, w_ref,
                       kv_ref, lff_ref, kff_ref, lfb_ref, kfb_ref, hv_ref, *, hw):
    x = x_ref[...]
    shift, scale = mod_ref[3:4, :], mod_ref[4:5, :]
    h = (_rms(x, gpre_ref[...]) * (1.0 + scale) + shift).astype(MXU_DTYPE)
    kvw = 2 * ATT_KV_WIDTH
    kv_ref[...] = jnp.dot(h, w_ref[:, :kvw], preferred_element_type=F32).astype(kv_ref.dtype)
    lf, kf = _forget(jnp.dot(h, w_ref[:, kvw:kvw + hw], preferred_element_type=F32), _lower_bound(lbf_ref))
    lff_ref[...] = lf
    kff_ref[...] = kf
    lf, kf = _forget(jnp.dot(h, w_ref[:, kvw + hw:kvw + 2 * hw], preferred_element_type=F32),
                     _lower_bound(lbb_ref))
    lfb_ref[...] = lf
    kfb_ref[...] = kf
    hv_ref[...] = jnp.dot(h, w_ref[:, kvw + 2 * hw:], preferred_element_type=F32)


def _inproj_ctx(xc, mod3, ctx_row, g_pre, lb_f, lb_b, w, tm):
    rows, d = xc.shape
    hw = lb_f.shape[1]
    row = lambda width: pl.BlockSpec((tm, width), lambda i: (i, 0))
    widths = [2 * ATT_KV_WIDTH, hw, hw, hw, hw, hw]
    dtypes = [MXU_DTYPE] + [F32] * 5
    return pl.pallas_call(
        functools.partial(_inproj_ctx_kernel, hw=hw),
        grid=(rows // tm,),
        in_specs=[row(d),
                  pl.BlockSpec((None, N_MOD, d), lambda i: (ctx_row, 0, 0)),
                  _resident((1, d)), _resident(lb_f.shape), _resident(lb_b.shape), _resident(w.shape)],
        out_specs=[row(wd) for wd in widths],
        out_shape=[jax.ShapeDtypeStruct((rows, wd), dt) for wd, dt in zip(widths, dtypes)],
        compiler_params=_params(1),
        name="inproj_ctx",
    )(xc, mod3, g_pre, lb_f, lb_b, w)


def _attn_kernel(sink_ref, q_ref, kv_ref, kvc_ref, o_ref, *, seq, band):
    step = pl.program_id(1)
    kvc = kvc_ref[...]
    low_lane = lax.broadcasted_iota(jnp.int32, (ATT_BLOCK, LANES), 1) < ATT_HEAD_DIM
    low_row = lax.broadcasted_iota(jnp.int32, (LANES, ATT_BLOCK), 0) < ATT_HEAD_DIM
    heads_per_group = 2 * ATT_SLABS

    def block_keys(qb):
        n = step * ATT_QBLOCKS + qb
        start = pl.multiple_of(jnp.clip(n * ATT_BLOCK - ATT_BLOCK, 0, seq - band), ATT_BLOCK)
        kvb = kv_ref[pl.ds(start, band), :]
        k_all = jnp.concatenate([kvb[:, :ATT_KV_WIDTH], kvc[:, :ATT_KV_WIDTH]], axis=0)
        v_t = jnp.concatenate([kvb[:, ATT_KV_WIDTH:], kvc[:, ATT_KV_WIDTH:]], axis=0).T
        kpos = start + lax.broadcasted_iota(jnp.int32, (band, ATT_BLOCK), 0)
        qpos = n * ATT_BLOCK + lax.broadcasted_iota(jnp.int32, (band, ATT_BLOCK), 1)
        valid = jnp.abs(kpos - qpos) <= WINDOW
        return k_all, v_t, jnp.concatenate([valid] * heads_per_group, axis=1)

    keys = [block_keys(qb) for qb in range(ATT_QBLOCKS)]

    def unit_scores(qb, m0):
        q_rows, sink_row = [], []
        for m in range(m0, m0 + ATT_SLABS):
            qm = q_ref[qb * ATT_BLOCK:(qb + 1) * ATT_BLOCK, m * LANES:(m + 1) * LANES]
            zero = jnp.zeros_like(qm)
            q_rows += [jnp.where(low_lane, qm, zero), jnp.where(low_lane, zero, qm)]
            sink_row += [jnp.full((1, ATT_BLOCK), sink_ref[m] * LOG2E, F32),
                         jnp.full((1, ATT_BLOCK), sink_ref[ATT_GROUP + m] * LOG2E, F32)]
        return _mm_nt(keys[qb][0], jnp.concatenate(q_rows, axis=0)), jnp.concatenate(sink_row, axis=1)

    units = [(qb, m0) for qb in range(ATT_QBLOCKS) for m0 in range(0, ATT_GROUP, ATT_SLABS)]
    pending = [unit_scores(*units[0])]
    for u, (qb, m0) in enumerate(units):
        if u + 1 < len(units):
            pending.append(unit_scores(*units[u + 1]))
        s_t, sink2 = pending[u]
        _, v_t, valid = keys[qb]
        s_t = jnp.concatenate([jnp.where(valid, s_t[:band], NEG_INF), s_t[band:]], axis=0)
        m = jnp.maximum(jnp.max(s_t, axis=0, keepdims=True), sink2)
        e = jnp.exp2(s_t - m)
        den = jnp.sum(e, axis=0, keepdims=True) + jnp.exp2(sink2 - m)
        o_t = _mm(v_t, e) * (1.0 / den)
        for j, m in enumerate(range(m0, m0 + ATT_SLABS)):
            o_lo = o_t[:, (2 * j) * ATT_BLOCK:(2 * j + 1) * ATT_BLOCK]
            o_hi = o_t[:, (2 * j + 1) * ATT_BLOCK:(2 * j + 2) * ATT_BLOCK]
            o_ref[qb * ATT_BLOCK:(qb + 1) * ATT_BLOCK, m * LANES:(m + 1) * LANES] = (
                jnp.where(low_row, o_lo, o_hi).T.astype(o_ref.dtype))


def _attention(sink, q, kv, kvc, batch):
    rows, _ = q.shape
    seq = rows // batch
    ctx_len = kvc.shape[0] // batch
    tq = ATT_QBLOCKS * ATT_BLOCK
    steps = seq // tq
    band = 3 * ATT_BLOCK
    return pl.pallas_call(
        functools.partial(_attn_kernel, seq=seq, band=band),
        grid=(batch, steps),
        in_specs=[pl.BlockSpec(memory_space=pltpu.SMEM),
                  pl.BlockSpec((tq, ATT_WIDTH), lambda b, n: (b * steps + n, 0)),
                  pl.BlockSpec((seq, 2 * ATT_KV_WIDTH), lambda b, n: (b, 0)),
                  pl.BlockSpec((ctx_len, 2 * ATT_KV_WIDTH), lambda b, n: (b, 0))],
        out_specs=pl.BlockSpec((tq, ATT_WIDTH), lambda b, n: (b * steps + n, 0)),
        out_shape=jax.ShapeDtypeStruct((rows, ATT_WIDTH), MXU_DTYPE),
        compiler_params=_params(2),
        name="attention",
    )(sink, q, kv, kvc)


def _cum_decay(lf, tri3):
    hi = lf.astype(jnp.bfloat16)
    r1 = lf - hi.astype(F32)
    mid = r1.astype(jnp.bfloat16)
    lo = (r1 - mid.astype(F32)).astype(jnp.bfloat16)
    return jnp.dot(tri3, jnp.concatenate([hi, mid, lo], axis=0), preferred_element_type=F32)


def _hg_state_update(st, k, v, b, b_end):
    k_dec = k * jnp.exp2(b_end - b)
    return st * jnp.exp2(b_end) + _mm_tn(v, k_dec)


def _hg_local(q, k, b, rev):
    c, sub = HG_CHUNK, HG_SUB
    nblk, half = c // sub, sub // 2
    q4, k4, b4 = (a.reshape(nblk, sub, LANES) for a in (q, k, b))
    tiles = []
    for s in range(sub):
        ks, bs = k4[:, s:s + 1, :], b4[:, s:s + 1, :]
        halves = []
        for hf in range(2):
            needed = (hf == 0 or s >= half) if rev else (hf == 1 or s < half)
            if needed:
                rows = slice(hf * half, (hf + 1) * half)
                halves.append(q4[:, rows] * ks * jnp.exp2(jnp.minimum(b4[:, rows] - bs, 0.0)))
            else:
                halves.append(jnp.zeros((nblk, half, LANES), F32))
        tiles.append(jnp.concatenate(halves, axis=1).reshape(c, LANES).astype(MXU_DTYPE))
    zcat = jnp.concatenate(tiles, axis=1)

    zero = jnp.zeros((sub, LANES), F32)

    def factors(q_blocks, k_blocks, anchor):
        ba = b[anchor:anchor + 1]
        qs = [q4[i] * jnp.exp2(b4[i] - ba) if i in q_blocks else zero for i in range(nblk)]
        ks = [k4[i] * jnp.exp2(ba - b4[i]) if i in k_blocks else zero for i in range(nblk)]
        return jnp.concatenate(qs, axis=0), jnp.concatenate(ks, axis=0)

    if rev:
        pairs = [factors((0,), (1,), sub), factors((2,), (3,), 3 * sub), factors((0, 1), (2, 3), 2 * sub)]
    else:
        pairs = [factors((1,), (0,), sub - 1), factors((3,), (2,), 3 * sub - 1), factors((2, 3), (0, 1), 2 * sub - 1)]
    qcat = jnp.concatenate([p[0] for p in pairs], axis=1).astype(MXU_DTYPE)
    kcat = jnp.concatenate([p[1] for p in pairs], axis=1).astype(MXU_DTYPE)
    return zcat, qcat, kcat


def _hgrn_kernel(q_ref, v_ref, lff_ref, kff_ref, lfb_ref, kfb_ref,
                 vc_ref, lffc_ref, kffc_ref, lfbc_ref, kfbc_ref, sel_ref, o_ref, sf_ref, sb_ref):
    c, sub = HG_CHUNK, HG_SUB
    seq, ctx_len = q_ref.shape[0], vc_ref.shape[0]
    n, nc = seq // c, ctx_len // c
    r = lax.broadcasted_iota(jnp.int32, (c, c), 0)
    s = lax.broadcasted_iota(jnp.int32, (c, c), 1)
    tri3_f = jnp.concatenate([(s <= r).astype(jnp.bfloat16)] * 3, axis=1)
    tri3_b = jnp.concatenate([(s >= r).astype(jnp.bfloat16)] * 3, axis=1)
    row = lax.broadcasted_iota(jnp.int32, (c, LANES), 0)
    lane = lax.broadcasted_iota(jnp.int32, (c, LANES), 1)
    own = (lane // sub) == (row // sub)
    dmask_f = own & ((lane % sub) <= (row % sub))
    dmask_b = own & ((lane % sub) >= (row % sub))
    pad = jnp.zeros((LANES - c, LANES), MXU_DTYPE)

    sf_ref[...] = jnp.zeros_like(sf_ref)
    sb_ref[...] = jnp.zeros_like(sb_ref)

    def ctx_step(i, carry):
        rf = pl.multiple_of(i * c, c)
        rb = pl.multiple_of((nc - 1 - i) * c, c)
        bf = _cum_decay(lffc_ref[pl.ds(rf, c), :], tri3_f)
        sf_ref[...] = _hg_state_update(sf_ref[...], kffc_ref[pl.ds(rf, c), :], vc_ref[pl.ds(rf, c), :],
                                       bf, bf[c - 1:c, :])
        bb = _cum_decay(lfbc_ref[pl.ds(rb, c), :], tri3_b)
        sb_ref[...] = _hg_state_update(sb_ref[...], kfbc_ref[pl.ds(rb, c), :], vc_ref[pl.ds(rb, c), :],
                                       bb, bb[0:1, :])
        return carry

    lax.fori_loop(0, nc, ctx_step, 0)

    def step(i, carry, accumulate):
        rows_f = pl.ds(pl.multiple_of(i * c, c), c)
        rows_b = pl.ds(pl.multiple_of((n - 1 - i) * c, c), c)
        dirs = []
        for rows, k_ref, lf_ref, tri3, rev in ((rows_f, kff_ref, lff_ref, tri3_f, False),
                                               (rows_b, kfb_ref, lfb_ref, tri3_b, True)):
            q, k, v = q_ref[rows, :], k_ref[rows, :], v_ref[rows, :]
            b = _cum_decay(lf_ref[rows, :], tri3)
            dirs.append((q, k, v, b) + _hg_local(q, k, b, rev))
        diag = jnp.dot(jnp.concatenate([dirs[0][4], dirs[1][4]], axis=0), sel_ref[...],
                       preferred_element_type=F32)
        for j, (st_ref, dmask, rows, rev) in enumerate(((sf_ref, dmask_f, rows_f, False),
                                                         (sb_ref, dmask_b, rows_b, True))):
            q, k, v, b, _, qcat, kcat = dirs[j]
            st = st_ref[...]
            b_end = b[0:1, :] if rev else b[c - 1:c, :]
            k_rows = jnp.concatenate([kcat, jnp.zeros((LANES - c, kcat.shape[1]), MXU_DTYPE)], axis=0)
            scores = jnp.where(dmask, diag[j * c:(j + 1) * c], 0.0) + _mm_nt(qcat, k_rows)
            o = (_mm_nt(q * jnp.exp2(b), st)
                 + _mm(scores, jnp.concatenate([v.astype(MXU_DTYPE), pad], axis=0)))
            st_ref[...] = _hg_state_update(st, k, v, b, b_end)
            if accumulate:
                o_ref[rows, :] += o
            else:
                o_ref[rows, :] = o
        return carry

    def steps(ii, carry, accumulate):
        for u in range(HG_UNROLL):
            carry = step(ii * HG_UNROLL + u, carry, accumulate)
        return carry

    half_trips = n // (2 * HG_UNROLL)
    lax.fori_loop(0, half_trips, functools.partial(steps, accumulate=False), 0)
    lax.fori_loop(half_trips, 2 * half_trips, functools.partial(steps, accumulate=True), 0)


def _hgrn(hq, hv, lff, kff, lfb, kfb, hvc, lffc, kffc, lfbc, kfbc, batch):
    rows, hw = hq.shape
    seq = rows // batch
    ctx_len = hvc.shape[0] // batch
    heads = hw // HG_EXPAND
    assert seq % (2 * HG_UNROLL * HG_CHUNK) == 0 and ctx_len % HG_CHUNK == 0
    src = jnp.arange(HG_SUB * LANES, dtype=jnp.int32)[:, None] // LANES
    sel = (src == jnp.arange(LANES, dtype=jnp.int32)[None, :] % HG_SUB).astype(MXU_DTYPE)
    lat = pl.BlockSpec((seq, HG_EXPAND), lambda b, h: (b, h))
    ctx = pl.BlockSpec((ctx_len, HG_EXPAND), lambda b, h: (b, h))
    return pl.pallas_call(
        _hgrn_kernel,
        grid=(batch, heads),
        in_specs=[lat] * 6 + [ctx] * 5 + [_resident(sel.shape)],
        out_specs=lat,
        out_shape=jax.ShapeDtypeStruct((rows, hw), F32),
        scratch_shapes=[pltpu.VMEM((HG_EXPAND, HG_EXPAND), F32)] * 2,
        compiler_params=_params(2),
        name="hgrn",
    )(hq, hv, lff, kff, lfb, kfb, hvc, lffc, kffc, lfbc, kfbc, sel)


def _merge_kernel(x_ref, mod_ref, gpost_ref, ghn_ref, oa_ref, oh_ref, og_ref, sga_ref, sgh_ref,
                  woa_ref, woh_ref, wout_ref, o_ref):
    gate = mod_ref[5:6, :]
    r = _rms(oh_ref[...], ghn_ref[...]) * og_ref[...]
    z = sga_ref[...] * _mm(oa_ref[...], woa_ref[...]) + sgh_ref[...] * _mm(r, woh_ref[...])
    y = _mm(z, wout_ref[...])
    o_ref[...] = x_ref[...] + gate * _rms(y, gpost_ref[...])


def _merge(x, mod3, tiles_per_batch, g_post, g_hnorm, oa, oh, og, sga, sgh, woa, woh, wout, tm):
    rows, d = x.shape
    row = lambda width: pl.BlockSpec((tm, width), lambda i: (i, 0))
    return pl.pallas_call(
        _merge_kernel,
        grid=(rows // tm,),
        in_specs=[row(d),
                  pl.BlockSpec((None, N_MOD, d), lambda i: (i // tiles_per_batch, 0, 0)),
                  _resident((1, d)), _resident(g_hnorm.shape),
                  row(oa.shape[1]), row(oh.shape[1]), row(og.shape[1]), row(d), row(d),
                  _resident(woa.shape), _resident(woh.shape), _resident(wout.shape)],
        out_specs=row(d),
        out_shape=jax.ShapeDtypeStruct((rows, d), F32),
        compiler_params=_params(1),
        name="merge",
    )(x, mod3, g_post, g_hnorm, oa, oh, og, sga, sgh, woa, woh, wout)


def _rope_tables(seq):
    quarter = ATT_HEAD_DIM // 4
    t = jnp.arange(seq, dtype=jnp.int32)
    freqs = ROPE_BASE ** (-jnp.arange(quarter, dtype=F32) / quarter)
    cos, sin = [], []
    for pos in (t // GRID_W, t % GRID_W):
        ang = pos.astype(F32)[:, None] * freqs[None, :]
        cos += [jnp.cos(ang), jnp.cos(ang)]
        sin += [-jnp.sin(ang), jnp.sin(ang)]
    reps = LANES // ATT_HEAD_DIM
    return jnp.concatenate(cos * reps, axis=1), jnp.concatenate(sin * reps, axis=1)


def _pair_heads(n_rows_or_cols):
    order = []
    for m in range(ATT_GROUP):
        for h in (m, ATT_GROUP + m):
            order += list(range(h * ATT_HEAD_DIM, (h + 1) * ATT_HEAD_DIM))
    assert len(order) == n_rows_or_cols
    return jnp.asarray(order, dtype=jnp.int32)


def kernel(x, c, ctx, c_ctx, w_ada, b_ada, norm_pre, norm_post, ffn1_w_in, ffn1_w_out, ffn2_w_in, ffn2_w_out,
           mix_w_in, attn_sink, hgrn_lb_fwd, hgrn_lb_bwd, hgrn_norm, w_o_attn, w_o_hgrn, w_out):
    batch, seq, d = x.shape
    ctx_len = ctx.shape[1]
    depth = w_ada.shape[0]
    assert depth == 1, "single-layer problem: the context stream is only read, never written back"
    hw = hgrn_norm.shape[1]
    tm = 256
    tiles_per_batch = seq // tm
    bf = lambda a: a.astype(MXU_DTYPE)

    mod_rows = -(-(batch + 1) // 8) * 8
    cs = jnp.concatenate([c, c_ctx[None, :], jnp.zeros((mod_rows - batch - 1, d), F32)], axis=0)
    mod3 = _modulation(cs, w_ada[0], b_ada[0]).reshape(mod_rows, N_MOD, d)
    ctx_row = batch

    xl = x.reshape(batch * seq, d)
    xc = ctx.reshape(batch * ctx_len, d)
    pre = [norm_pre[0, j][None, :] for j in range(3)]
    post = [norm_post[0, j][None, :] for j in range(3)]

    w1_in, w1_out = bf(ffn1_w_in[0]), bf(ffn1_w_out[0])
    xl = _ffn(xl, mod3, lambda i: i // tiles_per_batch, 0, pre[0], post[0], w1_in, w1_out, tm)
    xc = _ffn(xc, mod3, lambda i: ctx_row, 0, pre[0], post[0], w1_in, w1_out, tm)

    wm = mix_w_in[0]
    w_lat = bf(jnp.concatenate([jnp.take(wm[:, :ATT_WIDTH], _pair_heads(ATT_WIDTH), axis=1), wm[:, ATT_WIDTH:]],
                               axis=1))
    kv0, hg0 = ATT_WIDTH, ATT_WIDTH + 2 * ATT_KV_WIDTH + hw
    w_ctx = bf(jnp.concatenate([wm[:, kv0:kv0 + 2 * ATT_KV_WIDTH], wm[:, hg0:hg0 + 3 * hw]], axis=1))
    cos, sin = _rope_tables(seq)
    q, kv, hq, lff, kff, lfb, kfb, hv, og, sga, sgh = _inproj(
        xl, mod3, tiles_per_batch, pre[1], cos, sin, hgrn_lb_fwd, hgrn_lb_bwd, w_lat, tm)
    kvc, lffc, kffc, lfbc, kfbc, hvc = _inproj_ctx(xc, mod3, ctx_row, pre[1], hgrn_lb_fwd, hgrn_lb_bwd, w_ctx, tm)

    o_att = _attention(attn_sink[0], q, kv, kvc, batch)
    o_hg = _hgrn(hq, hv, lff, kff, lfb, kfb, hvc, lffc, kffc, lfbc, kfbc, batch)

    woa = bf(jnp.take(w_o_attn[0], _pair_heads(ATT_WIDTH), axis=0))
    xl = _merge(xl, mod3, tiles_per_batch, post[1], hgrn_norm, o_att, o_hg, og, sga, sgh,
                woa, bf(w_o_hgrn[0]), bf(w_out[0]), tm)

    xl = _ffn(xl, mod3, lambda i: i // tiles_per_batch, 6, pre[2], post[2], bf(ffn2_w_in[0]), bf(ffn2_w_out[0]), tm)
    return xl.reshape(batch, seq, d)
```

```python
import functools

import jax
import jax.numpy as jnp
from jax import lax
from jax.experimental import pallas as pl
from jax.experimental.pallas import tpu as pltpu

F32 = jnp.float32
MXU_DTYPE = jnp.bfloat16

N_MOD = 9
GRID_W = 64
ATT_HEADS = 8
ATT_KV_HEADS = 2
ATT_GROUP = ATT_HEADS // ATT_KV_HEADS
ATT_HEAD_DIM = 64
ATT_WIDTH = ATT_HEADS * ATT_HEAD_DIM
ATT_KV_WIDTH = ATT_KV_HEADS * ATT_HEAD_DIM
WINDOW = 128
ATT_BLOCK = 128
ATT_SLABS = 1
ATT_QBLOCKS = 2
ROPE_BASE = 10000.0
HG_EXPAND = 128
EPS = 1e-6
NEG_INF = -1e30
LOG2E = 1.4426950408889634

LANES = 128
HG_CHUNK = 64
HG_SUB = 16
HG_UNROLL = 4
VMEM_LIMIT = 56 * 1024 * 1024


def _sigmoid(x):
    return 1.0 / (1.0 + jnp.exp(-x))


def _silu(x):
    return x * _sigmoid(x)


def _rms(x, g):
    ms = jnp.mean(x * x, axis=-1, keepdims=True)
    return x * lax.rsqrt(ms + EPS) * g


def _mm(a, b):
    return jnp.dot(a.astype(MXU_DTYPE), b.astype(MXU_DTYPE), preferred_element_type=F32)


def _mm_nt(a, b):
    return lax.dot_general(a.astype(MXU_DTYPE), b.astype(MXU_DTYPE), (((1,), (1,)), ((), ())),
                           preferred_element_type=F32)


def _mm_tn(a, b):
    return lax.dot_general(a.astype(MXU_DTYPE), b.astype(MXU_DTYPE), (((0,), (0,)), ((), ())),
                           preferred_element_type=F32)


def _resident(shape):
    nd = len(shape)
    return pl.BlockSpec(shape, lambda *_: (0,) * nd, pipeline_mode=pl.Buffered(1))


def _params(n_axes):
    return pltpu.CompilerParams(dimension_semantics=("arbitrary",) * n_axes,
                                vmem_limit_bytes=VMEM_LIMIT)


def _mod_kernel(c_ref, w_ref, b_ref, o_ref):
    o_ref[...] = _mm(_silu(c_ref[...]), w_ref[...]) + b_ref[...]


def _modulation(cs, w_ada, b_ada):
    rows, d = cs.shape
    n = w_ada.shape[1]
    tn = n // 8
    return pl.pallas_call(
        _mod_kernel,
        grid=(n // tn,),
        in_specs=[pl.BlockSpec((rows, d), lambda j: (0, 0)),
                  pl.BlockSpec((d, tn), lambda j: (0, j)),
                  pl.BlockSpec((1, tn), lambda j: (0, j))],
        out_specs=pl.BlockSpec((rows, tn), lambda j: (0, j)),
        out_shape=jax.ShapeDtypeStruct((rows, n), F32),
        compiler_params=_params(1),
        name="modulation",
    )(cs, w_ada, b_ada.reshape(1, n))


def _ffn_kernel(x_ref, mod_ref, gpre_ref, gpost_ref, win_ref, wout_ref, o_ref, *, j0, d_ff):
    x = x_ref[...]
    shift, scale, gate = mod_ref[j0:j0 + 1, :], mod_ref[j0 + 1:j0 + 2, :], mod_ref[j0 + 2:j0 + 3, :]
    h = _rms(x, gpre_ref[...]) * (1.0 + scale) + shift
    gu = _mm(h, win_ref[...])
    act = _silu(gu[:, :d_ff]) * gu[:, d_ff:]
    y = _mm(act, wout_ref[...])
    o_ref[...] = x + 0.5 * gate * _rms(y, gpost_ref[...])


def _ffn(x, mod3, mod_row, j0, g_pre, g_post, w_in, w_out, tm):
    rows, d = x.shape
    d_ff = w_out.shape[0]
    return pl.pallas_call(
        functools.partial(_ffn_kernel, j0=j0, d_ff=d_ff),
        grid=(rows // tm,),
        in_specs=[pl.BlockSpec((tm, d), lambda i: (i, 0)),
                  pl.BlockSpec((None, N_MOD, d), lambda i: (mod_row(i), 0, 0)),
                  _resident((1, d)), _resident((1, d)),
                  _resident(w_in.shape), _resident(w_out.shape)],
        out_specs=pl.BlockSpec((tm, d), lambda i: (i, 0)),
        out_shape=jax.ShapeDtypeStruct((rows, d), F32),
        compiler_params=_params(1),
        name="ffn",
    )(x, mod3, g_pre, g_post, w_in, w_out)


def _lower_bound(lb_ref):
    a = lb_ref[...]
    e = jnp.exp(a - jnp.max(a, axis=0, keepdims=True))
    return e[0:1, :] / jnp.sum(e, axis=0, keepdims=True)


def _forget(z, lb):
    f = lb + (1.0 - lb) * _sigmoid(z)
    return jnp.log2(f), 1.0 - f


def _rope(x, cos, sin):
    w = x.shape[1]
    reps = w // LANES
    cos_w = jnp.concatenate([cos] * reps, axis=1) if reps > 1 else cos
    sin_w = jnp.concatenate([sin] * reps, axis=1) if reps > 1 else sin
    lane = lax.broadcasted_iota(jnp.int32, x.shape, 1)
    quarter = ATT_HEAD_DIM // 4
    first = (lane % (2 * quarter)) < quarter
    partner = jnp.where(first, pltpu.roll(x, w - quarter, 1), pltpu.roll(x, quarter, 1))
    return x * cos_w + partner * sin_w


def _inproj_kernel(x_ref, mod_ref, gpre_ref, cos_ref, sin_ref, lbf_ref, lbb_ref, w_ref,
                   q_ref, kv_ref, hq_ref, lff_ref, kff_ref, lfb_ref, kfb_ref, hv_ref, *, hw):
    x = x_ref[...]
    shift, scale = mod_ref[3:4, :], mod_ref[4:5, :]
    h = (_rms(x, gpre_ref[...]) * (1.0 + scale) + shift).astype(MXU_DTYPE)
    cos, sin = cos_ref[...], sin_ref[...]
    c0 = 0

    def proj(width):
        nonlocal c0
        out = jnp.dot(h, w_ref[:, c0:c0 + width], preferred_element_type=F32)
        c0 += width
        return out

    q = proj(ATT_WIDTH)
    q_ref[...] = (_rope(q, cos, sin) * (LOG2E * ATT_HEAD_DIM ** -0.5)).astype(q_ref.dtype)
    kv = proj(2 * ATT_KV_WIDTH)
    kv_ref[:, :ATT_KV_WIDTH] = _rope(kv[:, :ATT_KV_WIDTH], cos, sin).astype(kv_ref.dtype)
    kv_ref[:, ATT_KV_WIDTH:] = kv[:, ATT_KV_WIDTH:].astype(kv_ref.dtype)
    hq_ref[...] = _silu(proj(hw))
    lf, kf = _forget(proj(hw), _lower_bound(lbf_ref))
    lff_ref[...] = lf
    kff_ref[...] = kf
    lf, kf = _forget(proj(hw), _lower_bound(lbb_ref))
    lfb_ref[...] = lf
    kfb_ref[...] = kf
    hv_ref[...] = proj(hw)


def _inproj(x, mod3, tiles_per_batch, g_pre, cos, sin, lb_f, lb_b, w, tm):
    rows, d = x.shape
    hw = lb_f.shape[1]
    row = lambda width: pl.BlockSpec((tm, width), lambda i: (i, 0))
    widths = [ATT_WIDTH, 2 * ATT_KV_WIDTH, hw, hw, hw, hw, hw, hw]
    dtypes = [MXU_DTYPE, MXU_DTYPE] + [F32] * 6
    return pl.pallas_call(
        functools.partial(_inproj_kernel, hw=hw),
        grid=(rows // tm,),
        in_specs=[row(d),
                  pl.BlockSpec((None, N_MOD, d), lambda i: (i // tiles_per_batch, 0, 0)),
                  _resident((1, d)),
                  pl.BlockSpec((tm, LANES), lambda i: (i % tiles_per_batch, 0)),
                  pl.BlockSpec((tm, LANES), lambda i: (i % tiles_per_batch, 0)),
                  _resident(lb_f.shape), _resident(lb_b.shape), _resident(w.shape)],
        out_specs=[row(wd) for wd in widths],
        out_shape=[jax.ShapeDtypeStruct((rows, wd), dt) for wd, dt in zip(widths, dtypes)],
        compiler_params=_params(1),
        name="inproj",
    )(x, mod3, g_pre, cos, sin, lb_f, lb_b, w)


def _inproj_ctx_kernel(x_ref, mod_ref, gpre_ref, lbf_ref, lbb_ref, w_ref,
                       kv_ref, lff_ref, kff_ref, lfb_ref, kfb_ref, hv_ref, *, hw):
    x = x_ref[...]
    shift, scale = mod_ref[3:4, :], mod_ref[4:5, :]
    h = (_rms(x, gpre_ref[...]) * (1.0 + scale) + shift).astype(MXU_DTYPE)
    kvw = 2 * ATT_KV_WIDTH
    kv_ref[...] = jnp.dot(h, w_ref[:, :kvw], preferred_element_type=F32).astype(kv_ref.dtype)
    lf, kf = _forget(jnp.dot(h, w_ref[:, kvw:kvw + hw], preferred_element_type=F32), _lower_bound(lbf_ref))
    lff_ref[...] = lf
    kff_ref[...] = kf
    lf, kf = _forget(jnp.dot(h, w_ref[:, kvw + hw:kvw + 2 * hw], preferred_element_type=F32),
                     _lower_bound(lbb_ref))
    lfb_ref[...] = lf
    kfb_ref[...] = kf
    hv_ref[...] = jnp.dot(h, w_ref[:, kvw + 2 * hw:], preferred_element_type=F32)


def _inproj_ctx(xc, mod3, ctx_row, g_pre, lb_f, lb_b, w, tm):
    rows, d = xc.shape
    hw = lb_f.shape[1]
    row = lambda width: pl.BlockSpec((tm, width), lambda i: (i, 0))
    widths = [2 * ATT_KV_WIDTH, hw, hw, hw, hw, hw]
    dtypes = [MXU_DTYPE] + [F32] * 5
    return pl.pallas_call(
        functools.partial(_inproj_ctx_kernel, hw=hw),
        grid=(rows // tm,),
        in_specs=[row(d),
                  pl.BlockSpec((None, N_MOD, d), lambda i: (ctx_row, 0, 0)),
                  _resident((1, d)), _resident(lb_f.shape), _resident(lb_b.shape), _resident(w.shape)],
        out_specs=[row(wd) for wd in widths],
        out_shape=[jax.ShapeDtypeStruct((rows, wd), dt) for wd, dt in zip(widths, dtypes)],
        compiler_params=_params(1),
        name="inproj_ctx",
    )(xc, mod3, g_pre, lb_f, lb_b, w)


def _attn_kernel(sink_ref, q_ref, kv_ref, kvc_ref, o_ref, *, seq, band):
    step = pl.program_id(1)
    kvc = kvc_ref[...]
    low_lane = lax.broadcasted_iota(jnp.int32, (ATT_BLOCK, LANES), 1) < ATT_HEAD_DIM
    low_row = lax.broadcasted_iota(jnp.int32, (LANES, ATT_BLOCK), 0) < ATT_HEAD_DIM
    heads_per_group = 2 * ATT_SLABS

    def block_keys(qb):
        n = step * ATT_QBLOCKS + qb
        start = pl.multiple_of(jnp.clip(n * ATT_BLOCK - ATT_BLOCK, 0, seq - band), ATT_BLOCK)
        kvb = kv_ref[pl.ds(start, band), :]
        k_all = jnp.concatenate([kvb[:, :ATT_KV_WIDTH], kvc[:, :ATT_KV_WIDTH]], axis=0)
        v_t = jnp.concatenate([kvb[:, ATT_KV_WIDTH:], kvc[:, ATT_KV_WIDTH:]], axis=0).T
        kpos = start + lax.broadcasted_iota(jnp.int32, (band, ATT_BLOCK), 0)
        qpos = n * ATT_BLOCK + lax.broadcasted_iota(jnp.int32, (band, ATT_BLOCK), 1)
        valid = jnp.abs(kpos - qpos) <= WINDOW
        return k_all, v_t, jnp.concatenate([valid] * heads_per_group, axis=1)

    keys = [block_keys(qb) for qb in range(ATT_QBLOCKS)]

    def unit_scores(qb, m0):
        q_rows, sink_row = [], []
        for m in range(m0, m0 + ATT_SLABS):
            qm = q_ref[qb * ATT_BLOCK:(qb + 1) * ATT_BLOCK, m * LANES:(m + 1) * LANES]
            zero = jnp.zeros_like(qm)
            q_rows += [jnp.where(low_lane, qm, zero), jnp.where(low_lane, zero, qm)]
            sink_row += [jnp.full((1, ATT_BLOCK), sink_ref[m] * LOG2E, F32),
                         jnp.full((1, ATT_BLOCK), sink_ref[ATT_GROUP + m] * LOG2E, F32)]
        return _mm_nt(keys[qb][0], jnp.concatenate(q_rows, axis=0)), jnp.concatenate(sink_row, axis=1)

    units = [(qb, m0) for qb in range(ATT_QBLOCKS) for m0 in range(0, ATT_GROUP, ATT_SLABS)]
    pending = [unit_scores(*units[0])]
    for u, (qb, m0) in enumerate(units):
        if u + 1 < len(units):
            pending.append(unit_scores(*units[u + 1]))
        s_t, sink2 = pending[u]
        _, v_t, valid = keys[qb]
        s_t = jnp.concatenate([jnp.where(valid, s_t[:band], NEG_INF), s_t[band:]], axis=0)
        m = jnp.maximum(jnp.max(s_t, axis=0, keepdims=True), sink2)
        e = jnp.exp2(s_t - m)
        den = jnp.sum(e, axis=0, keepdims=True) + jnp.exp2(sink2 - m)
        o_t = _mm(v_t, e) * (1.0 / den)
        for j, m in enumerate(range(m0, m0 + ATT_SLABS)):
            o_lo = o_t[:, (2 * j) * ATT_BLOCK:(2 * j + 1) * ATT_BLOCK]
            o_hi = o_t[:, (2 * j + 1) * ATT_BLOCK:(2 * j + 2) * ATT_BLOCK]
            o_ref[qb * ATT_BLOCK:(qb + 1) * ATT_BLOCK, m * LANES:(m + 1) * LANES] = (
                jnp.where(low_row, o_lo, o_hi).T.astype(o_ref.dtype))


def _attention(sink, q, kv, kvc, batch):
    rows, _ = q.shape
    seq = rows // batch
    ctx_len = kvc.shape[0] // batch
    tq = ATT_QBLOCKS * ATT_BLOCK
    steps = seq // tq
    band = 3 * ATT_BLOCK
    return pl.pallas_call(
        functools.partial(_attn_kernel, seq=seq, band=band),
        grid=(batch, steps),
        in_specs=[pl.BlockSpec(memory_space=pltpu.SMEM),
                  pl.BlockSpec((tq, ATT_WIDTH), lambda b, n: (b * steps + n, 0)),
                  pl.BlockSpec((seq, 2 * ATT_KV_WIDTH), lambda b, n: (b, 0)),
                  pl.BlockSpec((ctx_len, 2 * ATT_KV_WIDTH), lambda b, n: (b, 0))],
        out_specs=pl.BlockSpec((tq, ATT_WIDTH), lambda b, n: (b * steps + n, 0)),
        out_shape=jax.ShapeDtypeStruct((rows, ATT_WIDTH), MXU_DTYPE),
        compiler_params=_params(2),
        name="attention",
    )(sink, q, kv, kvc)


def _cum_decay(lf, tri3):
    hi = lf.astype(jnp.bfloat16)
    r1 = lf - hi.astype(F32)
    mid = r1.astype(jnp.bfloat16)
    lo = (r1 - mid.astype(F32)).astype(jnp.bfloat16)
    return jnp.dot(tri3, jnp.concatenate([hi, mid, lo], axis=0), preferred_element_type=F32)


def _hg_state_update(st, k, v, b, b_end):
    k_dec = k * jnp.exp2(b_end - b)
    return st * jnp.exp2(b_end) + _mm_tn(v, k_dec)


def _hg_local(q, k, b, rev):
    c, sub = HG_CHUNK, HG_SUB
    nblk, half = c // sub, sub // 2
    q4, k4, b4 = (a.reshape(nblk, sub, LANES) for a in (q, k, b))
    tiles = []
    for s in range(sub):
        ks, bs = k4[:, s:s + 1, :], b4[:, s:s + 1, :]
        halves = []
        for hf in range(2):
            needed = (hf == 0 or s >= half) if rev else (hf == 1 or s < half)
            if needed:
                rows = slice(hf * half, (hf + 1) * half)
                halves.append(q4[:, rows] * ks * jnp.exp2(jnp.minimum(b4[:, rows] - bs, 0.0)))
            else:
                halves.append(jnp.zeros((nblk, half, LANES), F32))
        tiles.append(jnp.concatenate(halves, axis=1).reshape(c, LANES).astype(MXU_DTYPE))
    zcat = jnp.concatenate(tiles, axis=1)

    zero = jnp.zeros((sub, LANES), F32)

    def factors(q_blocks, k_blocks, anchor):
        ba = b[anchor:anchor + 1]
        qs = [q4[i] * jnp.exp2(b4[i] - ba) if i in q_blocks else zero for i in range(nblk)]
        ks = [k4[i] * jnp.exp2(ba - b4[i]) if i in k_blocks else zero for i in range(nblk)]
        return jnp.concatenate(qs, axis=0), jnp.concatenate(ks, axis=0)

    if rev:
        pairs = [factors((0,), (1,), sub), factors((2,), (3,), 3 * sub), factors((0, 1), (2, 3), 2 * sub)]
    else:
        pairs = [factors((1,), (0,), sub - 1), factors((3,), (2,), 3 * sub - 1), factors((2, 3), (0, 1), 2 * sub - 1)]
    qcat = jnp.concatenate([p[0] for p in pairs], axis=1).astype(MXU_DTYPE)
    kcat = jnp.concatenate([p[1] for p in pairs], axis=1).astype(MXU_DTYPE)
    return zcat, qcat, kcat


def _hgrn_kernel(q_ref, v_ref, lff_ref, kff_ref, lfb_ref, kfb_ref,
                 vc_ref, lffc_ref, kffc_ref, lfbc_ref, kfbc_ref, sel_ref, o_ref,
                 sf_ref, sb_ref, z_ref, qc_ref, kc_ref, b_ref):
    c, sub = HG_CHUNK, HG_SUB
    seq, ctx_len = q_ref.shape[0], vc_ref.shape[0]
    n, nc = seq // c, ctx_len // c
    r = lax.broadcasted_iota(jnp.int32, (c, c), 0)
    s = lax.broadcasted_iota(jnp.int32, (c, c), 1)
    tri3_f = jnp.concatenate([(s <= r).astype(jnp.bfloat16)] * 3, axis=1)
    tri3_b = jnp.concatenate([(s >= r).astype(jnp.bfloat16)] * 3, axis=1)
    row = lax.broadcasted_iota(jnp.int32, (c, LANES), 0)
    lane = lax.broadcasted_iota(jnp.int32, (c, LANES), 1)
    own = (lane // sub) == (row // sub)
    dmask_f = own & ((lane % sub) <= (row % sub))
    dmask_b = own & ((lane % sub) >= (row % sub))
    pad = jnp.zeros((LANES - c, LANES), MXU_DTYPE)

    sf_ref[...] = jnp.zeros_like(sf_ref)
    sb_ref[...] = jnp.zeros_like(sb_ref)

    def ctx_step(i, carry):
        rf = pl.multiple_of(i * c, c)
        rb = pl.multiple_of((nc - 1 - i) * c, c)
        bf = _cum_decay(lffc_ref[pl.ds(rf, c), :], tri3_f)
        sf_ref[...] = _hg_state_update(sf_ref[...], kffc_ref[pl.ds(rf, c), :], vc_ref[pl.ds(rf, c), :],
                                       bf, bf[c - 1:c, :])
        bb = _cum_decay(lfbc_ref[pl.ds(rb, c), :], tri3_b)
        sb_ref[...] = _hg_state_update(sb_ref[...], kfbc_ref[pl.ds(rb, c), :], vc_ref[pl.ds(rb, c), :],
                                       bb, bb[0:1, :])
        return carry

    lax.fori_loop(0, nc, ctx_step, 0)

    dirs = ((kff_ref, lff_ref, tri3_f, sf_ref, dmask_f, False),
            (kfb_ref, lfb_ref, tri3_b, sb_ref, dmask_b, True))

    def chunk_rows(i, rev):
        return pl.ds(pl.multiple_of(((n - 1 - i) if rev else i) * c, c), c)

    def cum_decays(i):
        return [_cum_decay(lf_ref[chunk_rows(i, rev), :], tri3) for _, lf_ref, tri3, _, _, rev in dirs]

    def stage_local(i, bs, slot):
        for j, (k_ref, _, _, _, _, rev) in enumerate(dirs):
            rows = chunk_rows(i, rev)
            zcat, qcat, kcat = _hg_local(q_ref[rows, :], k_ref[rows, :], bs[j], rev)
            z_ref[slot, j * c:(j + 1) * c, :] = zcat
            qc_ref[slot, j] = qcat
            kc_ref[slot, j, 0:c, :] = kcat
            b_ref[slot, j] = bs[j]

    def step(i, slot, accumulate):
        nxt = jnp.minimum(i + 1, n - 1)
        b_next = cum_decays(nxt)
        diag = jnp.dot(z_ref[slot], sel_ref[...], preferred_element_type=F32)
        started = []
        for j, (k_ref, _, _, st_ref, _, rev) in enumerate(dirs):
            rows = chunk_rows(i, rev)
            q, k, v, b = q_ref[rows, :], k_ref[rows, :], v_ref[rows, :], b_ref[slot, j]
            st = st_ref[...]
            b_end = b[0:1, :] if rev else b[c - 1:c, :]
            off = _mm_nt(qc_ref[slot, j], kc_ref[slot, j])
            o_inter = _mm_nt(q * jnp.exp2(b), st)
            started.append((rows, v, off, o_inter, _hg_state_update(st, k, v, b, b_end)))
        stage_local(nxt, b_next, 1 - slot)
        for j, (_, _, _, st_ref, dmask, _) in enumerate(dirs):
            rows, v, off, o_inter, st_new = started[j]
            scores = jnp.where(dmask, diag[j * c:(j + 1) * c], 0.0) + off
            o = o_inter + _mm(scores, jnp.concatenate([v.astype(MXU_DTYPE), pad], axis=0))
            st_ref[...] = st_new
            if accumulate:
                o_ref[rows, :] += o
            else:
                o_ref[rows, :] = o

    kc_ref[...] = jnp.zeros_like(kc_ref)
    stage_local(0, cum_decays(0), 0)

    def steps(ii, carry, accumulate):
        for u in range(HG_UNROLL):
            step(ii * HG_UNROLL + u, u % 2, accumulate)
        return carry

    half_trips = n // (2 * HG_UNROLL)
    lax.fori_loop(0, half_trips, functools.partial(steps, accumulate=False), 0)
    lax.fori_loop(half_trips, 2 * half_trips, functools.partial(steps, accumulate=True), 0)


def _hgrn(hq, hv, lff, kff, lfb, kfb, hvc, lffc, kffc, lfbc, kfbc, batch):
    rows, hw = hq.shape
    seq = rows // batch
    ctx_len = hvc.shape[0] // batch
    heads = hw // HG_EXPAND
    assert seq % (2 * HG_UNROLL * HG_CHUNK) == 0 and ctx_len % HG_CHUNK == 0 and HG_UNROLL % 2 == 0
    src = jnp.arange(HG_SUB * LANES, dtype=jnp.int32)[:, None] // LANES
    sel = (src == jnp.arange(LANES, dtype=jnp.int32)[None, :] % HG_SUB).astype(MXU_DTYPE)
    lat = pl.BlockSpec((seq, HG_EXPAND), lambda b, h: (b, h))
    ctx = pl.BlockSpec((ctx_len, HG_EXPAND), lambda b, h: (b, h))
    return pl.pallas_call(
        _hgrn_kernel,
        grid=(batch, heads),
        in_specs=[lat] * 6 + [ctx] * 5 + [_resident(sel.shape)],
        out_specs=lat,
        out_shape=jax.ShapeDtypeStruct((rows, hw), F32),
        scratch_shapes=[pltpu.VMEM((HG_EXPAND, HG_EXPAND), F32)] * 2 + [
            pltpu.VMEM((2, 2 * HG_CHUNK, HG_SUB * LANES), MXU_DTYPE),
            pltpu.VMEM((2, 2, HG_CHUNK, 3 * LANES), MXU_DTYPE),
            pltpu.VMEM((2, 2, LANES, 3 * LANES), MXU_DTYPE),
            pltpu.VMEM((2, 2, HG_CHUNK, LANES), F32)],
        compiler_params=_params(2),
        name="hgrn",
    )(hq, hv, lff, kff, lfb, kfb, hvc, lffc, kffc, lfbc, kfbc, sel)


def _tail_kernel(x_ref, mod_ref, pre_ref, post_ref, ghn_ref, oa_ref, oh_ref,
                 wg_ref, woa_ref, woh_ref, wout_ref, win_ref, wo2_ref, o_ref, *, hw, d_ff):
    x = x_ref[...]
    d = x.shape[1]
    mod = lambda j: mod_ref[j:j + 1, :]
    h = _rms(x, pre_ref[1:2, :]) * (1.0 + mod(4)) + mod(3)
    g = _mm(h, wg_ref[...])
    r = _rms(oh_ref[...], ghn_ref[...]) * _silu(g[:, :hw])
    z = (_sigmoid(g[:, hw:hw + d]) * _mm(oa_ref[...], woa_ref[...])
         + _sigmoid(g[:, hw + d:]) * _mm(r, woh_ref[...]))
    x = x + mod(5) * _rms(_mm(z, wout_ref[...]), post_ref[1:2, :])
    h = _rms(x, pre_ref[2:3, :]) * (1.0 + mod(7)) + mod(6)
    gu = _mm(h, win_ref[...])
    y = _mm(_silu(gu[:, :d_ff]) * gu[:, d_ff:], wo2_ref[...])
    o_ref[...] = x + 0.5 * mod(8) * _rms(y, post_ref[2:3, :])


def _tail(x, mod3, tiles_per_batch, pre, post, g_hnorm, oa, oh, wg, woa, woh, wout, w_in, w_out, tm):
    rows, d = x.shape
    row = lambda width: pl.BlockSpec((tm, width), lambda i: (i, 0))
    return pl.pallas_call(
        functools.partial(_tail_kernel, hw=g_hnorm.shape[1], d_ff=w_out.shape[0]),
        grid=(rows // tm,),
        in_specs=[row(d),
                  pl.BlockSpec((None, N_MOD, d), lambda i: (i // tiles_per_batch, 0, 0)),
                  _resident(pre.shape), _resident(post.shape), _resident(g_hnorm.shape),
                  row(oa.shape[1]), row(oh.shape[1])]
                 + [_resident(w.shape) for w in (wg, woa, woh, wout, w_in, w_out)],
        out_specs=row(d),
        out_shape=jax.ShapeDtypeStruct((rows, d), F32),
        compiler_params=_params(1),
        name="merge_ffn",
    )(x, mod3, pre, post, g_hnorm, oa, oh, wg, woa, woh, wout, w_in, w_out)


def _rope_tables(seq):
    quarter = ATT_HEAD_DIM // 4
    t = jnp.arange(seq, dtype=jnp.int32)
    freqs = ROPE_BASE ** (-jnp.arange(quarter, dtype=F32) / quarter)
    cos, sin = [], []
    for pos in (t // GRID_W, t % GRID_W):
        ang = pos.astype(F32)[:, None] * freqs[None, :]
        cos += [jnp.cos(ang), jnp.cos(ang)]
        sin += [-jnp.sin(ang), jnp.sin(ang)]
    reps = LANES // ATT_HEAD_DIM
    return jnp.concatenate(cos * reps, axis=1), jnp.concatenate(sin * reps, axis=1)


def _pair_heads(n_rows_or_cols):
    order = []
    for m in range(ATT_GROUP):
        for h in (m, ATT_GROUP + m):
            order += list(range(h * ATT_HEAD_DIM, (h + 1) * ATT_HEAD_DIM))
    assert len(order) == n_rows_or_cols
    return jnp.asarray(order, dtype=jnp.int32)


def kernel(x, c, ctx, c_ctx, w_ada, b_ada, norm_pre, norm_post, ffn1_w_in, ffn1_w_out, ffn2_w_in, ffn2_w_out,
           mix_w_in, attn_sink, hgrn_lb_fwd, hgrn_lb_bwd, hgrn_norm, w_o_attn, w_o_hgrn, w_out):
    batch, seq, d = x.shape
    ctx_len = ctx.shape[1]
    depth = w_ada.shape[0]
    assert depth == 1, "single-layer problem: the context stream is only read, never written back"
    hw = hgrn_norm.shape[1]
    tm = 256
    tiles_per_batch = seq // tm
    bf = lambda a: a.astype(MXU_DTYPE)

    mod_rows = -(-(batch + 1) // 8) * 8
    cs = jnp.concatenate([c, c_ctx[None, :], jnp.zeros((mod_rows - batch - 1, d), F32)], axis=0)
    mod3 = _modulation(cs, w_ada[0], b_ada[0]).reshape(mod_rows, N_MOD, d)
    ctx_row = batch

    xl = x.reshape(batch * seq, d)
    xc = ctx.reshape(batch * ctx_len, d)
    pre = [norm_pre[0, j][None, :] for j in range(3)]
    post = [norm_post[0, j][None, :] for j in range(3)]

    w1_in, w1_out = bf(ffn1_w_in[0]), bf(ffn1_w_out[0])
    xl = _ffn(xl, mod3, lambda i: i // tiles_per_batch, 0, pre[0], post[0], w1_in, w1_out, tm)
    xc = _ffn(xc, mod3, lambda i: ctx_row, 0, pre[0], post[0], w1_in, w1_out, tm)

    wm = mix_w_in[0]
    kv0 = ATT_WIDTH
    hg0 = kv0 + 2 * ATT_KV_WIDTH
    gate0 = hg0 + 4 * hw
    w_lat = bf(jnp.concatenate([jnp.take(wm[:, :kv0], _pair_heads(ATT_WIDTH), axis=1), wm[:, kv0:gate0]], axis=1))
    w_ctx = bf(jnp.concatenate([wm[:, kv0:hg0], wm[:, hg0 + hw:gate0]], axis=1))
    cos, sin = _rope_tables(seq)
    q, kv, hq, lff, kff, lfb, kfb, hv = _inproj(
        xl, mod3, tiles_per_batch, pre[1], cos, sin, hgrn_lb_fwd, hgrn_lb_bwd, w_lat, tm)
    kvc, lffc, kffc, lfbc, kfbc, hvc = _inproj_ctx(xc, mod3, ctx_row, pre[1], hgrn_lb_fwd, hgrn_lb_bwd, w_ctx, tm)

    o_att = _attention(attn_sink[0], q, kv, kvc, batch)
    o_hg = _hgrn(hq, hv, lff, kff, lfb, kfb, hvc, lffc, kffc, lfbc, kfbc, batch)

    woa = bf(jnp.take(w_o_attn[0], _pair_heads(ATT_WIDTH), axis=0))
    xl = _tail(xl, mod3, tiles_per_batch, norm_pre[0], norm_post[0], hgrn_norm, o_att, o_hg,
               bf(wm[:, gate0:]), woa, bf(w_o_hgrn[0]), bf(w_out[0]), bf(ffn2_w_in[0]), bf(ffn2_w_out[0]), tm)
    return xl.reshape(batch, seq, d)
```

```python
import functools

import jax
import jax.numpy as jnp
from jax import lax
from jax.experimental import pallas as pl
from jax.experimental.pallas import tpu as pltpu

F32 = jnp.float32
MXU_DTYPE = jnp.bfloat16

N_MOD = 9
GRID_W = 64
ATT_HEADS = 8
ATT_KV_HEADS = 2
ATT_GROUP = ATT_HEADS // ATT_KV_HEADS
ATT_HEAD_DIM = 64
ATT_WIDTH = ATT_HEADS * ATT_HEAD_DIM
ATT_KV_WIDTH = ATT_KV_HEADS * ATT_HEAD_DIM
WINDOW = 128
ATT_BLOCK = 128
ATT_SLABS = 1
ATT_QBLOCKS = 2
ROPE_BASE = 10000.0
HG_EXPAND = 128
EPS = 1e-6
NEG_INF = -1e30
LOG2E = 1.4426950408889634

LANES = 128
HG_CHUNK = 64
HG_SUB = 16
HG_UNROLL = 4
N_OFF_SEGMENTS = 3
HG_SAFE_LOG2 = 120.0
VMEM_LIMIT = 56 * 1024 * 1024


def _sigmoid(x):
    return 1.0 / (1.0 + jnp.exp(-x))


def _silu(x):
    return x * _sigmoid(x)


def _rms(x, g):
    ms = jnp.mean(x * x, axis=-1, keepdims=True)
    return x * lax.rsqrt(ms + EPS) * g


def _mm(a, b):
    return jnp.dot(a.astype(MXU_DTYPE), b.astype(MXU_DTYPE), preferred_element_type=F32)


def _mm_nt(a, b):
    return lax.dot_general(a.astype(MXU_DTYPE), b.astype(MXU_DTYPE), (((1,), (1,)), ((), ())),
                           preferred_element_type=F32)


def _mm_tn(a, b):
    return lax.dot_general(a.astype(MXU_DTYPE), b.astype(MXU_DTYPE), (((0,), (0,)), ((), ())),
                           preferred_element_type=F32)


def _resident(shape):
    nd = len(shape)
    return pl.BlockSpec(shape, lambda *_: (0,) * nd, pipeline_mode=pl.Buffered(1))


def _params(n_axes):
    return pltpu.CompilerParams(dimension_semantics=("arbitrary",) * n_axes,
                                vmem_limit_bytes=VMEM_LIMIT)


def _mod_kernel(c_ref, w_ref, b_ref, o_ref):
    o_ref[...] = _mm(_silu(c_ref[...]), w_ref[...]) + b_ref[...]


def _modulation(cs, w_ada, b_ada):
    rows, d = cs.shape
    n = w_ada.shape[1]
    tn = n // 8
    return pl.pallas_call(
        _mod_kernel,
        grid=(n // tn,),
        in_specs=[pl.BlockSpec((rows, d), lambda j: (0, 0)),
                  pl.BlockSpec((d, tn), lambda j: (0, j)),
                  pl.BlockSpec((1, tn), lambda j: (0, j))],
        out_specs=pl.BlockSpec((rows, tn), lambda j: (0, j)),
        out_shape=jax.ShapeDtypeStruct((rows, n), F32),
        compiler_params=_params(1),
        name="modulation",
    )(cs, w_ada, b_ada.reshape(1, n))


def _ffn_kernel(x_ref, mod_ref, gpre_ref, gpost_ref, win_ref, wout_ref, o_ref, *, j0, d_ff):
    x = x_ref[...]
    shift, scale, gate = mod_ref[j0:j0 + 1, :], mod_ref[j0 + 1:j0 + 2, :], mod_ref[j0 + 2:j0 + 3, :]
    h = _rms(x, gpre_ref[...]) * (1.0 + scale) + shift
    gu = _mm(h, win_ref[...])
    act = _silu(gu[:, :d_ff]) * gu[:, d_ff:]
    y = _mm(act, wout_ref[...])
    o_ref[...] = x + 0.5 * gate * _rms(y, gpost_ref[...])


def _ffn(x, mod3, mod_row, j0, g_pre, g_post, w_in, w_out, tm):
    rows, d = x.shape
    d_ff = w_out.shape[0]
    return pl.pallas_call(
        functools.partial(_ffn_kernel, j0=j0, d_ff=d_ff),
        grid=(rows // tm,),
        in_specs=[pl.BlockSpec((tm, d), lambda i: (i, 0)),
                  pl.BlockSpec((None, N_MOD, d), lambda i: (mod_row(i), 0, 0)),
                  _resident((1, d)), _resident((1, d)),
                  _resident(w_in.shape), _resident(w_out.shape)],
        out_specs=pl.BlockSpec((tm, d), lambda i: (i, 0)),
        out_shape=jax.ShapeDtypeStruct((rows, d), F32),
        compiler_params=_params(1),
        name="ffn",
    )(x, mod3, g_pre, g_post, w_in, w_out)


def _lower_bound(lb_ref):
    a = lb_ref[...]
    e = jnp.exp(a - jnp.max(a, axis=0, keepdims=True))
    return e[0:1, :] / jnp.sum(e, axis=0, keepdims=True)


def _forget(z, lb):
    f = lb + (1.0 - lb) * _sigmoid(z)
    return jnp.log2(f), 1.0 - f


def _rope(x, cos, sin):
    w = x.shape[1]
    reps = w // LANES
    cos_w = jnp.concatenate([cos] * reps, axis=1) if reps > 1 else cos
    sin_w = jnp.concatenate([sin] * reps, axis=1) if reps > 1 else sin
    lane = lax.broadcasted_iota(jnp.int32, x.shape, 1)
    quarter = ATT_HEAD_DIM // 4
    first = (lane % (2 * quarter)) < quarter
    partner = jnp.where(first, pltpu.roll(x, w - quarter, 1), pltpu.roll(x, quarter, 1))
    return x * cos_w + partner * sin_w


def _inproj_kernel(x_ref, mod_ref, gpre_ref, cos_ref, sin_ref, lbf_ref, lbb_ref, w_ref,
                   q_ref, kv_ref, hq_ref, lff_ref, kff_ref, lfb_ref, kfb_ref, hv_ref, *, hw):
    x = x_ref[...]
    shift, scale = mod_ref[3:4, :], mod_ref[4:5, :]
    h = (_rms(x, gpre_ref[...]) * (1.0 + scale) + shift).astype(MXU_DTYPE)
    cos, sin = cos_ref[...], sin_ref[...]
    c0 = 0

    def proj(width):
        nonlocal c0
        out = jnp.dot(h, w_ref[:, c0:c0 + width], preferred_element_type=F32)
        c0 += width
        return out

    q = proj(ATT_WIDTH)
    q_ref[...] = (_rope(q, cos, sin) * (LOG2E * ATT_HEAD_DIM ** -0.5)).astype(q_ref.dtype)
    kv = proj(2 * ATT_KV_WIDTH)
    kv_ref[:, :ATT_KV_WIDTH] = _rope(kv[:, :ATT_KV_WIDTH], cos, sin).astype(kv_ref.dtype)
    kv_ref[:, ATT_KV_WIDTH:] = kv[:, ATT_KV_WIDTH:].astype(kv_ref.dtype)
    hq_ref[...] = _silu(proj(hw))
    lf, kf = _forget(proj(hw), _lower_bound(lbf_ref))
    lff_ref[...] = lf
    kff_ref[...] = kf
    lf, kf = _forget(proj(hw), _lower_bound(lbb_ref))
    lfb_ref[...] = lf
    kfb_ref[...] = kf
    hv_ref[...] = proj(hw)


def _inproj(x, mod3, tiles_per_batch, g_pre, cos, sin, lb_f, lb_b, w, tm):
    rows, d = x.shape
    hw = lb_f.shape[1]
    row = lambda width: pl.BlockSpec((tm, width), lambda i: (i, 0))
    widths = [ATT_WIDTH, 2 * ATT_KV_WIDTH, hw, hw, hw, hw, hw, hw]
    dtypes = [MXU_DTYPE, MXU_DTYPE] + [F32] * 6
    return pl.pallas_call(
        functools.partial(_inproj_kernel, hw=hw),
        grid=(rows // tm,),
        in_specs=[row(d),
                  pl.BlockSpec((None, N_MOD, d), lambda i: (i // tiles_per_batch, 0, 0)),
                  _resident((1, d)),
                  pl.BlockSpec((tm, LANES), lambda i: (i % tiles_per_batch, 0)),
                  pl.BlockSpec((tm, LANES), lambda i: (i % tiles_per_batch, 0)),
                  _resident(lb_f.shape), _resident(lb_b.shape), _resident(w.shape)],
        out_specs=[row(wd) for wd in widths],
        out_shape=[jax.ShapeDtypeStruct((rows, wd), dt) for wd, dt in zip(widths, dtypes)],
        compiler_params=_params(1),
        name="inproj",
    )(x, mod3, g_pre, cos, sin, lb_f, lb_b, w)


def _inproj_ctx_kernel(x_ref, mod_ref, gpre_ref, lbf_ref, lbb_ref, w_ref,
                       kv_ref, lff_ref, kff_ref, lfb_ref, kfb_ref, hv_ref, *, hw):
    x = x_ref[...]
    shift, scale = mod_ref[3:4, :], mod_ref[4:5, :]
    h = (_rms(x, gpre_ref[...]) * (1.0 + scale) + shift).astype(MXU_DTYPE)
    kvw = 2 * ATT_KV_WIDTH
    kv_ref[...] = jnp.dot(h, w_ref[:, :kvw], preferred_element_type=F32).astype(kv_ref.dtype)
    lf, kf = _forget(jnp.dot(h, w_ref[:, kvw:kvw + hw], preferred_element_type=F32), _lower_bound(lbf_ref))
    lff_ref[...] = lf
    kff_ref[...] = kf
    lf, kf = _forget(jnp.dot(h, w_ref[:, kvw + hw:kvw + 2 * hw], preferred_element_type=F32),
                     _lower_bound(lbb_ref))
    lfb_ref[...] = lf
    kfb_ref[...] = kf
    hv_ref[...] = jnp.dot(h, w_ref[:, kvw + 2 * hw:], preferred_element_type=F32)


def _inproj_ctx(xc, mod3, ctx_row, g_pre, lb_f, lb_b, w, tm):
    rows, d = xc.shape
    hw = lb_f.shape[1]
    row = lambda width: pl.BlockSpec((tm, width), lambda i: (i, 0))
    widths = [2 * ATT_KV_WIDTH, hw, hw, hw, hw, hw]
    dtypes = [MXU_DTYPE] + [F32] * 5
    return pl.pallas_call(
        functools.partial(_inproj_ctx_kernel, hw=hw),
        grid=(rows // tm,),
        in_specs=[row(d),
                  pl.BlockSpec((None, N_MOD, d), lambda i: (ctx_row, 0, 0)),
                  _resident((1, d)), _resident(lb_f.shape), _resident(lb_b.shape), _resident(w.shape)],
        out_specs=[row(wd) for wd in widths],
        out_shape=[jax.ShapeDtypeStruct((rows, wd), dt) for wd, dt in zip(widths, dtypes)],
        compiler_params=_params(1),
        name="inproj_ctx",
    )(xc, mod3, g_pre, lb_f, lb_b, w)


def _attn_kernel(sink_ref, q_ref, kv_ref, kvc_ref, o_ref, *, seq, band):
    step = pl.program_id(1)
    kvc = kvc_ref[...]
    low_lane = lax.broadcasted_iota(jnp.int32, (ATT_BLOCK, LANES), 1) < ATT_HEAD_DIM
    low_row = lax.broadcasted_iota(jnp.int32, (LANES, ATT_BLOCK), 0) < ATT_HEAD_DIM
    heads_per_group = 2 * ATT_SLABS

    def block_keys(qb):
        n = step * ATT_QBLOCKS + qb
        start = pl.multiple_of(jnp.clip(n * ATT_BLOCK - ATT_BLOCK, 0, seq - band), ATT_BLOCK)
        kvb = kv_ref[pl.ds(start, band), :]
        k_all = jnp.concatenate([kvb[:, :ATT_KV_WIDTH], kvc[:, :ATT_KV_WIDTH]], axis=0)
        v_t = jnp.concatenate([kvb[:, ATT_KV_WIDTH:], kvc[:, ATT_KV_WIDTH:]], axis=0).T
        kpos = start + lax.broadcasted_iota(jnp.int32, (band, ATT_BLOCK), 0)
        qpos = n * ATT_BLOCK + lax.broadcasted_iota(jnp.int32, (band, ATT_BLOCK), 1)
        valid = jnp.abs(kpos - qpos) <= WINDOW
        return k_all, v_t, jnp.concatenate([valid] * heads_per_group, axis=1)

    keys = [block_keys(qb) for qb in range(ATT_QBLOCKS)]

    def unit_scores(qb, m0):
        q_rows, sink_row = [], []
        for m in range(m0, m0 + ATT_SLABS):
            qm = q_ref[qb * ATT_BLOCK:(qb + 1) * ATT_BLOCK, m * LANES:(m + 1) * LANES]
            zero = jnp.zeros_like(qm)
            q_rows += [jnp.where(low_lane, qm, zero), jnp.where(low_lane, zero, qm)]
            sink_row += [jnp.full((1, ATT_BLOCK), sink_ref[m] * LOG2E, F32),
                         jnp.full((1, ATT_BLOCK), sink_ref[ATT_GROUP + m] * LOG2E, F32)]
        return _mm_nt(keys[qb][0], jnp.concatenate(q_rows, axis=0)), jnp.concatenate(sink_row, axis=1)

    units = [(qb, m0) for qb in range(ATT_QBLOCKS) for m0 in range(0, ATT_GROUP, ATT_SLABS)]
    pending = [unit_scores(*units[0])]
    for u, (qb, m0) in enumerate(units):
        if u + 1 < len(units):
            pending.append(unit_scores(*units[u + 1]))
        s_t, sink2 = pending[u]
        _, v_t, valid = keys[qb]
        s_t = jnp.concatenate([jnp.where(valid, s_t[:band], NEG_INF), s_t[band:]], axis=0)
        m = jnp.maximum(jnp.max(s_t, axis=0, keepdims=True), sink2)
        e = jnp.exp2(s_t - m)
        den = jnp.sum(e, axis=0, keepdims=True) + jnp.exp2(sink2 - m)
        o_t = _mm(v_t, e) * (1.0 / den)
        for j, m in enumerate(range(m0, m0 + ATT_SLABS)):
            o_lo = o_t[:, (2 * j) * ATT_BLOCK:(2 * j + 1) * ATT_BLOCK]
            o_hi = o_t[:, (2 * j + 1) * ATT_BLOCK:(2 * j + 2) * ATT_BLOCK]
            o_ref[qb * ATT_BLOCK:(qb + 1) * ATT_BLOCK, m * LANES:(m + 1) * LANES] = (
                jnp.where(low_row, o_lo, o_hi).T.astype(o_ref.dtype))


def _attention(sink, q, kv, kvc, batch):
    rows, _ = q.shape
    seq = rows // batch
    ctx_len = kvc.shape[0] // batch
    tq = ATT_QBLOCKS * ATT_BLOCK
    steps = seq // tq
    band = 3 * ATT_BLOCK
    return pl.pallas_call(
        functools.partial(_attn_kernel, seq=seq, band=band),
        grid=(batch, steps),
        in_specs=[pl.BlockSpec(memory_space=pltpu.SMEM),
                  pl.BlockSpec((tq, ATT_WIDTH), lambda b, n: (b * steps + n, 0)),
                  pl.BlockSpec((seq, 2 * ATT_KV_WIDTH), lambda b, n: (b, 0)),
                  pl.BlockSpec((ctx_len, 2 * ATT_KV_WIDTH), lambda b, n: (b, 0))],
        out_specs=pl.BlockSpec((tq, ATT_WIDTH), lambda b, n: (b * steps + n, 0)),
        out_shape=jax.ShapeDtypeStruct((rows, ATT_WIDTH), MXU_DTYPE),
        compiler_params=_params(2),
        name="attention",
    )(sink, q, kv, kvc)


def _cum_decay(lf, tri3):
    hi = lf.astype(jnp.bfloat16)
    r1 = lf - hi.astype(F32)
    mid = r1.astype(jnp.bfloat16)
    lo = (r1 - mid.astype(F32)).astype(jnp.bfloat16)
    return jnp.dot(tri3, jnp.concatenate([hi, mid, lo], axis=0), preferred_element_type=F32)


def _hg_state_update(st, k, v, b, b_end):
    k_dec = k * jnp.exp2(b_end - b)
    return st * jnp.exp2(b_end) + _mm_tn(v, k_dec)


def _hg_diag_tiles(q4, k4, b4, rev):
    nblk, sub, _ = q4.shape
    half = sub // 2
    tiles = []
    for s in range(sub):
        ks, bs = k4[:, s:s + 1, :], b4[:, s:s + 1, :]
        halves = []
        for hf in range(2):
            needed = (hf == 0 or s >= half) if rev else (hf == 1 or s < half)
            if needed:
                rows = slice(hf * half, (hf + 1) * half)
                halves.append(q4[:, rows] * ks * jnp.exp2(jnp.minimum(b4[:, rows] - bs, 0.0)))
            else:
                halves.append(jnp.zeros((nblk, half, LANES), F32))
        tiles.append(jnp.concatenate(halves, axis=1).reshape(nblk * sub, LANES).astype(MXU_DTYPE))
    return jnp.concatenate(tiles, axis=1)


def _hg_local(q, k, b, rev, exact):
    c, sub = HG_CHUNK, HG_SUB
    nblk = c // sub
    q4, k4, b4 = (a.reshape(nblk, sub, LANES) for a in (q, k, b))
    zero = jnp.zeros((sub, LANES), F32)

    def factors(q_blocks, k_blocks, anchor):
        ba = b[anchor:anchor + 1]
        qs = [q4[i] * jnp.exp2(b4[i] - ba) if i in q_blocks else zero for i in range(nblk)]
        ks = [k4[i] * jnp.exp2(ba - b4[i]) if i in k_blocks else zero for i in range(nblk)]
        return jnp.concatenate(qs, axis=0), jnp.concatenate(ks, axis=0)

    if rev:
        pairs = [factors((0,), (1,), sub), factors((2,), (3,), 3 * sub), factors((0, 1), (2, 3), 2 * sub)]
    else:
        pairs = [factors((1,), (0,), sub - 1), factors((3,), (2,), 3 * sub - 1), factors((2, 3), (0, 1), 2 * sub - 1)]
    if not exact:
        pairs += [factors((i,), (i,), i * sub + (sub - 1 if rev else 0)) for i in range(nblk)]
    qcat = jnp.concatenate([p[0] for p in pairs], axis=1).astype(MXU_DTYPE)
    kcat = jnp.concatenate([p[1] for p in pairs], axis=1).astype(MXU_DTYPE)
    return (_hg_diag_tiles(q4, k4, b4, rev) if exact else None), qcat, kcat


def _hgrn_kernel(q_ref, v_ref, lff_ref, kff_ref, lfb_ref, kfb_ref,
                 vc_ref, lffc_ref, kffc_ref, lfbc_ref, kfbc_ref, sel_ref, o_ref,
                 sf_ref, sb_ref, z_ref, qc_ref, kc_ref, b_ref):
    c, sub = HG_CHUNK, HG_SUB
    seq, ctx_len = q_ref.shape[0], vc_ref.shape[0]
    n, nc = seq // c, ctx_len // c
    r = lax.broadcasted_iota(jnp.int32, (c, c), 0)
    s = lax.broadcasted_iota(jnp.int32, (c, c), 1)
    tri3_f = jnp.concatenate([(s <= r).astype(jnp.bfloat16)] * 3, axis=1)
    tri3_b = jnp.concatenate([(s >= r).astype(jnp.bfloat16)] * 3, axis=1)
    row = lax.broadcasted_iota(jnp.int32, (c, LANES), 0)
    lane = lax.broadcasted_iota(jnp.int32, (c, LANES), 1)
    own = (lane // sub) == (row // sub)
    dmask_f = own & ((lane % sub) <= (row % sub))
    dmask_b = own & ((lane % sub) >= (row % sub))
    pad = jnp.zeros((LANES - c, LANES), MXU_DTYPE)

    sf_ref[...] = jnp.zeros_like(sf_ref)
    sb_ref[...] = jnp.zeros_like(sb_ref)

    def ctx_step(i, carry):
        rf = pl.multiple_of(i * c, c)
        rb = pl.multiple_of((nc - 1 - i) * c, c)
        bf = _cum_decay(lffc_ref[pl.ds(rf, c), :], tri3_f)
        sf_ref[...] = _hg_state_update(sf_ref[...], kffc_ref[pl.ds(rf, c), :], vc_ref[pl.ds(rf, c), :],
                                       bf, bf[c - 1:c, :])
        bb = _cum_decay(lfbc_ref[pl.ds(rb, c), :], tri3_b)
        sb_ref[...] = _hg_state_update(sb_ref[...], kfbc_ref[pl.ds(rb, c), :], vc_ref[pl.ds(rb, c), :],
                                       bb, bb[0:1, :])
        return carry

    lax.fori_loop(0, nc, ctx_step, 0)

    dirs = ((kff_ref, lff_ref, tri3_f, sf_ref, dmask_f, False),
            (kfb_ref, lfb_ref, tri3_b, sb_ref, dmask_b, True))

    def chunk_rows(i, rev):
        return pl.ds(pl.multiple_of(((n - 1 - i) if rev else i) * c, c), c)

    def cum_decays(i):
        return [_cum_decay(lf_ref[chunk_rows(i, rev), :], tri3) for _, lf_ref, tri3, _, _, rev in dirs]

    def stage_local(i, bs, slot, exact):
        for j, (k_ref, _, _, _, _, rev) in enumerate(dirs):
            rows = chunk_rows(i, rev)
            zcat, qcat, kcat = _hg_local(q_ref[rows, :], k_ref[rows, :], bs[j], rev, exact)
            if exact:
                z_ref[slot, j * c:(j + 1) * c, :] = zcat
            qc_ref[slot, j, :, 0:qcat.shape[1]] = qcat
            kc_ref[slot, j, 0:c, 0:kcat.shape[1]] = kcat
            b_ref[slot, j] = bs[j]

    def step(i, slot, accumulate, exact):
        width = (N_OFF_SEGMENTS if exact else N_OFF_SEGMENTS + c // sub) * LANES
        nxt = jnp.minimum(i + 1, n - 1)
        b_next = cum_decays(nxt)
        if exact:
            diag = jnp.dot(z_ref[slot], sel_ref[...], preferred_element_type=F32)
        started = []
        for j, (k_ref, _, _, st_ref, _, rev) in enumerate(dirs):
            rows = chunk_rows(i, rev)
            q, k, v, b = q_ref[rows, :], k_ref[rows, :], v_ref[rows, :], b_ref[slot, j]
            st = st_ref[...]
            b_end = b[0:1, :] if rev else b[c - 1:c, :]
            factored = _mm_nt(qc_ref[slot, j, :, 0:width], kc_ref[slot, j, :, 0:width])
            o_inter = _mm_nt(q * jnp.exp2(b), st)
            started.append((rows, v, factored, o_inter, _hg_state_update(st, k, v, b, b_end)))
        stage_local(nxt, b_next, 1 - slot, exact)
        for j, (_, _, _, st_ref, dmask, rev) in enumerate(dirs):
            rows, v, factored, o_inter, st_new = started[j]
            if exact:
                scores = jnp.where(dmask, diag[j * c:(j + 1) * c], 0.0) + factored
            else:
                scores = jnp.where((lane >= row) if rev else (lane <= row), factored, 0.0)
            o = o_inter + _mm(scores, jnp.concatenate([v.astype(MXU_DTYPE), pad], axis=0))
            st_ref[...] = st_new
            if accumulate:
                o_ref[rows, :] += o
            else:
                o_ref[rows, :] = o

    def scan(exact):
        stage_local(0, cum_decays(0), 0, exact)

        def steps(ii, carry, accumulate):
            for u in range(HG_UNROLL):
                step(ii * HG_UNROLL + u, u % 2, accumulate, exact)
            return carry

        half_trips = n // (2 * HG_UNROLL)
        lax.fori_loop(0, half_trips, functools.partial(steps, accumulate=False), 0)
        lax.fori_loop(half_trips, 2 * half_trips, functools.partial(steps, accumulate=True), 0)

    kc_ref[...] = jnp.zeros_like(kc_ref)
    steepest = jnp.minimum(jnp.min(lff_ref[...]), jnp.min(lfb_ref[...]))
    lax.cond(steepest * (sub - 1) >= -HG_SAFE_LOG2, lambda: scan(False), lambda: scan(True))


def _hgrn(hq, hv, lff, kff, lfb, kfb, hvc, lffc, kffc, lfbc, kfbc, batch):
    rows, hw = hq.shape
    seq = rows // batch
    ctx_len = hvc.shape[0] // batch
    heads = hw // HG_EXPAND
    assert seq % (2 * HG_UNROLL * HG_CHUNK) == 0 and ctx_len % HG_CHUNK == 0 and HG_UNROLL % 2 == 0
    n_seg = N_OFF_SEGMENTS + HG_CHUNK // HG_SUB
    src = jnp.arange(HG_SUB * LANES, dtype=jnp.int32)[:, None] // LANES
    sel = (src == jnp.arange(LANES, dtype=jnp.int32)[None, :] % HG_SUB).astype(MXU_DTYPE)
    lat = pl.BlockSpec((seq, HG_EXPAND), lambda b, h: (b, h))
    ctx = pl.BlockSpec((ctx_len, HG_EXPAND), lambda b, h: (b, h))
    return pl.pallas_call(
        _hgrn_kernel,
        grid=(batch, heads),
        in_specs=[lat] * 6 + [ctx] * 5 + [_resident(sel.shape)],
        out_specs=lat,
        out_shape=jax.ShapeDtypeStruct((rows, hw), F32),
        scratch_shapes=[pltpu.VMEM((HG_EXPAND, HG_EXPAND), F32)] * 2 + [
            pltpu.VMEM((2, 2 * HG_CHUNK, HG_SUB * LANES), MXU_DTYPE),
            pltpu.VMEM((2, 2, HG_CHUNK, n_seg * LANES), MXU_DTYPE),
            pltpu.VMEM((2, 2, LANES, n_seg * LANES), MXU_DTYPE),
            pltpu.VMEM((2, 2, HG_CHUNK, LANES), F32)],
        compiler_params=_params(2),
        name="hgrn",
    )(hq, hv, lff, kff, lfb, kfb, hvc, lffc, kffc, lfbc, kfbc, sel)


def _tail_kernel(x_ref, mod_ref, pre_ref, post_ref, ghn_ref, oa_ref, oh_ref,
                 wg_ref, woa_ref, woh_ref, wout_ref, win_ref, wo2_ref, o_ref, *, hw, d_ff):
    x = x_ref[...]
    d = x.shape[1]
    mod = lambda j: mod_ref[j:j + 1, :]
    h = _rms(x, pre_ref[1:2, :]) * (1.0 + mod(4)) + mod(3)
    g = _mm(h, wg_ref[...])
    r = _rms(oh_ref[...], ghn_ref[...]) * _silu(g[:, :hw])
    z = (_sigmoid(g[:, hw:hw + d]) * _mm(oa_ref[...], woa_ref[...])
         + _sigmoid(g[:, hw + d:]) * _mm(r, woh_ref[...]))
    x = x + mod(5) * _rms(_mm(z, wout_ref[...]), post_ref[1:2, :])
    h = _rms(x, pre_ref[2:3, :]) * (1.0 + mod(7)) + mod(6)
    gu = _mm(h, win_ref[...])
    y = _mm(_silu(gu[:, :d_ff]) * gu[:, d_ff:], wo2_ref[...])
    o_ref[...] = x + 0.5 * mod(8) * _rms(y, post_ref[2:3, :])


def _tail(x, mod3, tiles_per_batch, pre, post, g_hnorm, oa, oh, wg, woa, woh, wout, w_in, w_out, tm):
    rows, d = x.shape
    row = lambda width: pl.BlockSpec((tm, width), lambda i: (i, 0))
    return pl.pallas_call(
        functools.partial(_tail_kernel, hw=g_hnorm.shape[1], d_ff=w_out.shape[0]),
        grid=(rows // tm,),
        in_specs=[row(d),
                  pl.BlockSpec((None, N_MOD, d), lambda i: (i // tiles_per_batch, 0, 0)),
                  _resident(pre.shape), _resident(post.shape), _resident(g_hnorm.shape),
                  row(oa.shape[1]), row(oh.shape[1])]
                 + [_resident(w.shape) for w in (wg, woa, woh, wout, w_in, w_out)],
        out_specs=row(d),
        out_shape=jax.ShapeDtypeStruct((rows, d), F32),
        compiler_params=_params(1),
        name="merge_ffn",
    )(x, mod3, pre, post, g_hnorm, oa, oh, wg, woa, woh, wout, w_in, w_out)


def _rope_tables(seq):
    quarter = ATT_HEAD_DIM // 4
    t = jnp.arange(seq, dtype=jnp.int32)
    freqs = ROPE_BASE ** (-jnp.arange(quarter, dtype=F32) / quarter)
    cos, sin = [], []
    for pos in (t // GRID_W, t % GRID_W):
        ang = pos.astype(F32)[:, None] * freqs[None, :]
        cos += [jnp.cos(ang), jnp.cos(ang)]
        sin += [-jnp.sin(ang), jnp.sin(ang)]
    reps = LANES // ATT_HEAD_DIM
    return jnp.concatenate(cos * reps, axis=1), jnp.concatenate(sin * reps, axis=1)


def _pair_heads(n_rows_or_cols):
    order = []
    for m in range(ATT_GROUP):
        for h in (m, ATT_GROUP + m):
            order += list(range(h * ATT_HEAD_DIM, (h + 1) * ATT_HEAD_DIM))
    assert len(order) == n_rows_or_cols
    return jnp.asarray(order, dtype=jnp.int32)


def kernel(x, c, ctx, c_ctx, w_ada, b_ada, norm_pre, norm_post, ffn1_w_in, ffn1_w_out, ffn2_w_in, ffn2_w_out,
           mix_w_in, attn_sink, hgrn_lb_fwd, hgrn_lb_bwd, hgrn_norm, w_o_attn, w_o_hgrn, w_out):
    batch, seq, d = x.shape
    ctx_len = ctx.shape[1]
    depth = w_ada.shape[0]
    assert depth == 1, "single-layer problem: the context stream is only read, never written back"
    hw = hgrn_norm.shape[1]
    tm = 256
    tiles_per_batch = seq // tm
    bf = lambda a: a.astype(MXU_DTYPE)

    mod_rows = -(-(batch + 1) // 8) * 8
    cs = jnp.concatenate([c, c_ctx[None, :], jnp.zeros((mod_rows - batch - 1, d), F32)], axis=0)
    mod3 = _modulation(cs, w_ada[0], b_ada[0]).reshape(mod_rows, N_MOD, d)
    ctx_row = batch

    xl = x.reshape(batch * seq, d)
    xc = ctx.reshape(batch * ctx_len, d)
    pre = [norm_pre[0, j][None, :] for j in range(3)]
    post = [norm_post[0, j][None, :] for j in range(3)]

    w1_in, w1_out = bf(ffn1_w_in[0]), bf(ffn1_w_out[0])
    xl = _ffn(xl, mod3, lambda i: i // tiles_per_batch, 0, pre[0], post[0], w1_in, w1_out, tm)
    xc = _ffn(xc, mod3, lambda i: ctx_row, 0, pre[0], post[0], w1_in, w1_out, tm)

    wm = mix_w_in[0]
    kv0 = ATT_WIDTH
    hg0 = kv0 + 2 * ATT_KV_WIDTH
    gate0 = hg0 + 4 * hw
    w_lat = bf(jnp.concatenate([jnp.take(wm[:, :kv0], _pair_heads(ATT_WIDTH), axis=1), wm[:, kv0:gate0]], axis=1))
    w_ctx = bf(jnp.concatenate([wm[:, kv0:hg0], wm[:, hg0 + hw:gate0]], axis=1))
    cos, sin = _rope_tables(seq)
    q, kv, hq, lff, kff, lfb, kfb, hv = _inproj(
        xl, mod3, tiles_per_batch, pre[1], cos, sin, hgrn_lb_fwd, hgrn_lb_bwd, w_lat, tm)
    kvc, lffc, kffc, lfbc, kfbc, hvc = _inproj_ctx(xc, mod3, ctx_row, pre[1], hgrn_lb_fwd, hgrn_lb_bwd, w_ctx, tm)

    o_att = _attention(attn_sink[0], q, kv, kvc, batch)
    o_hg = _hgrn(hq, hv, lff, kff, lfb, kfb, hvc, lffc, kffc, lfbc, kfbc, batch)

    woa = bf(jnp.take(w_o_attn[0], _pair_heads(ATT_WIDTH), axis=0))
    xl = _tail(xl, mod3, tiles_per_batch, norm_pre[0], norm_post[0], hgrn_norm, o_att, o_hg,
               bf(wm[:, gate0:]), woa, bf(w_o_hgrn[0]), bf(w_out[0]), bf(ffn2_w_in[0]), bf(ffn2_w_out[0]), tm)
    return xl.reshape(batch, seq, d)
```

```python
import functools

import jax
import jax.numpy as jnp
from jax import lax
from jax.experimental import pallas as pl
from jax.experimental.pallas import tpu as pltpu

F32 = jnp.float32
MXU_DTYPE = jnp.bfloat16

N_MOD = 9
GRID_W = 64
ATT_HEADS = 8
ATT_KV_HEADS = 2
ATT_GROUP = ATT_HEADS // ATT_KV_HEADS
ATT_HEAD_DIM = 64
ATT_WIDTH = ATT_HEADS * ATT_HEAD_DIM
ATT_KV_WIDTH = ATT_KV_HEADS * ATT_HEAD_DIM
WINDOW = 128
ATT_BLOCK = 128
ATT_SLABS = 1
ATT_QBLOCKS = 2
ROPE_BASE = 10000.0
HG_EXPAND = 128
FFN_SPLIT = 2
ROW_HALF = 256
EPS = 1e-6
NEG_INF = -1e30
LOG2E = 1.4426950408889634

LANES = 128
HG_CHUNK = 64
HG_SUB = 16
HG_UNROLL = 8
HG_UNROLL_EXACT = 2
N_OFF_SEGMENTS = 3
HG_SAFE_LOG2 = 120.0
VMEM_LIMIT = 56 * 1024 * 1024


def _sigmoid(x):
    return 1.0 / (1.0 + jnp.exp(-x))


def _silu(x):
    return x * _sigmoid(x)


def _rms(x, g):
    ms = jnp.mean(x * x, axis=-1, keepdims=True)
    return x * lax.rsqrt(ms + EPS) * g


def _mm(a, b):
    return jnp.dot(a.astype(MXU_DTYPE), b.astype(MXU_DTYPE), preferred_element_type=F32)


def _mm_nt(a, b):
    return lax.dot_general(a.astype(MXU_DTYPE), b.astype(MXU_DTYPE), (((1,), (1,)), ((), ())),
                           preferred_element_type=F32)


def _mm_tn(a, b):
    return lax.dot_general(a.astype(MXU_DTYPE), b.astype(MXU_DTYPE), (((0,), (0,)), ((), ())),
                           preferred_element_type=F32)


def _resident(shape):
    nd = len(shape)
    return pl.BlockSpec(shape, lambda *_: (0,) * nd, pipeline_mode=pl.Buffered(1))


def _params(n_axes):
    return pltpu.CompilerParams(dimension_semantics=("arbitrary",) * n_axes,
                                vmem_limit_bytes=VMEM_LIMIT)


def _mod_kernel(c_ref, w_ref, b_ref, o_ref):
    o_ref[...] = _mm(_silu(c_ref[...]), w_ref[...]) + b_ref[...]


def _modulation(cs, w_ada, b_ada):
    rows, d = cs.shape
    n = w_ada.shape[1]
    tn = n // 8
    return pl.pallas_call(
        _mod_kernel,
        grid=(n // tn,),
        in_specs=[pl.BlockSpec((rows, d), lambda j: (0, 0)),
                  pl.BlockSpec((d, tn), lambda j: (0, j)),
                  pl.BlockSpec((1, tn), lambda j: (0, j))],
        out_specs=pl.BlockSpec((rows, tn), lambda j: (0, j)),
        out_shape=jax.ShapeDtypeStruct((rows, n), F32),
        compiler_params=_params(1),
        name="modulation",
    )(cs, w_ada, b_ada.reshape(1, n))


def _ffn_kernel(x_ref, mod_ref, gpre_ref, gpost_ref, win_ref, wout_ref, o_ref, *, j0, d_ff):
    shift, scale, gate = mod_ref[j0:j0 + 1, :], mod_ref[j0 + 1:j0 + 2, :], mod_ref[j0 + 2:j0 + 3, :]
    th = x_ref.shape[0] // FFN_SPLIT
    halves = [slice(i * th, (i + 1) * th) for i in range(FFN_SPLIT)]
    xs = [x_ref[r, :] for r in halves]
    hs = [(_rms(x, gpre_ref[...]) * (1.0 + scale) + shift).astype(MXU_DTYPE) for x in xs]
    gus = [_mm(h, win_ref[...]) for h in hs]
    acts = [(_silu(gu[:, :d_ff]) * gu[:, d_ff:]).astype(MXU_DTYPE) for gu in gus]
    ys = [_mm(act, wout_ref[...]) for act in acts]
    for r, x, y in zip(halves, xs, ys):
        o_ref[r, :] = x + 0.5 * gate * _rms(y, gpost_ref[...])


def _ffn(x, mod3, mod_row, j0, g_pre, g_post, w_in, w_out, tm):
    rows, d = x.shape
    assert rows % tm == 0, (rows, tm)
    d_ff = w_out.shape[0]
    return pl.pallas_call(
        functools.partial(_ffn_kernel, j0=j0, d_ff=d_ff),
        grid=(rows // tm,),
        in_specs=[pl.BlockSpec((tm, d), lambda i: (i, 0)),
                  pl.BlockSpec((None, N_MOD, d), lambda i: (mod_row(i), 0, 0)),
                  _resident((1, d)), _resident((1, d)),
                  _resident(w_in.shape), _resident(w_out.shape)],
        out_specs=pl.BlockSpec((tm, d), lambda i: (i, 0)),
        out_shape=jax.ShapeDtypeStruct((rows, d), F32),
        compiler_params=_params(1),
        name="ffn",
    )(x, mod3, g_pre, g_post, w_in, w_out)


def _lower_bound(lb_ref):
    a = lb_ref[...]
    e = jnp.exp(a - jnp.max(a, axis=0, keepdims=True))
    return e[0:1, :] / jnp.sum(e, axis=0, keepdims=True)


def _forget(z, lb):
    f = lb + (1.0 - lb) * _sigmoid(z)
    return jnp.log2(f), 1.0 - f


def _rope(x, cos, sin):
    w = x.shape[1]
    reps = w // LANES
    cos_w = jnp.concatenate([cos] * reps, axis=1) if reps > 1 else cos
    sin_w = jnp.concatenate([sin] * reps, axis=1) if reps > 1 else sin
    lane = lax.broadcasted_iota(jnp.int32, x.shape, 1)
    quarter = ATT_HEAD_DIM // 4
    first = (lane % (2 * quarter)) < quarter
    partner = jnp.where(first, pltpu.roll(x, w - quarter, 1), pltpu.roll(x, quarter, 1))
    return x * cos_w + partner * sin_w


def _inproj_kernel(x_ref, mod_ref, gpre_ref, cos_ref, sin_ref, lbf_ref, lbb_ref, w_ref,
                   q_ref, kv_ref, hq_ref, lff_ref, kff_ref, lfb_ref, kfb_ref, hv_ref, *, hw):
    shift, scale = mod_ref[3:4, :], mod_ref[4:5, :]
    th = x_ref.shape[0] // FFN_SPLIT
    halves = [slice(i * th, (i + 1) * th) for i in range(FFN_SPLIT)]
    hs = [(_rms(x_ref[r, :], gpre_ref[...]) * (1.0 + scale) + shift).astype(MXU_DTYPE) for r in halves]
    c0 = 0

    def proj(width):
        nonlocal c0
        outs = [jnp.dot(h, w_ref[:, c0:c0 + width], preferred_element_type=F32) for h in hs]
        c0 += width
        return zip(halves, outs)

    for r, q in proj(ATT_WIDTH):
        q_ref[r, :] = (_rope(q, cos_ref[r, :], sin_ref[r, :]) * (LOG2E * ATT_HEAD_DIM ** -0.5)).astype(q_ref.dtype)
    for r, kv in proj(2 * ATT_KV_WIDTH):
        kv_ref[r, :ATT_KV_WIDTH] = _rope(kv[:, :ATT_KV_WIDTH], cos_ref[r, :], sin_ref[r, :]).astype(kv_ref.dtype)
        kv_ref[r, ATT_KV_WIDTH:] = kv[:, ATT_KV_WIDTH:].astype(kv_ref.dtype)
    for r, z in proj(hw):
        hq_ref[r, :] = _silu(z).astype(hq_ref.dtype)
    for lb_ref, lf_ref, kf_ref in ((lbf_ref, lff_ref, kff_ref), (lbb_ref, lfb_ref, kfb_ref)):
        for r, z in proj(hw):
            lf, kf = _forget(z, _lower_bound(lb_ref))
            lf_ref[r, :] = lf
            kf_ref[r, :] = kf.astype(kf_ref.dtype)
    for r, z in proj(hw):
        hv_ref[r, :] = z.astype(hv_ref.dtype)


def _inproj(x, mod3, tiles_per_batch, g_pre, cos, sin, lb_f, lb_b, w, tm):
    rows, d = x.shape
    assert rows % tm == 0, (rows, tm)
    hw = lb_f.shape[1]
    row = lambda width: pl.BlockSpec((tm, width), lambda i: (i, 0))
    widths = [ATT_WIDTH, 2 * ATT_KV_WIDTH, hw, hw, hw, hw, hw, hw]
    dtypes = [MXU_DTYPE, MXU_DTYPE, MXU_DTYPE, F32, MXU_DTYPE, F32, MXU_DTYPE, MXU_DTYPE]
    return pl.pallas_call(
        functools.partial(_inproj_kernel, hw=hw),
        grid=(rows // tm,),
        in_specs=[row(d),
                  pl.BlockSpec((None, N_MOD, d), lambda i: (i // tiles_per_batch, 0, 0)),
                  _resident((1, d)),
                  pl.BlockSpec((tm, LANES), lambda i: (i % tiles_per_batch, 0)),
                  pl.BlockSpec((tm, LANES), lambda i: (i % tiles_per_batch, 0)),
                  _resident(lb_f.shape), _resident(lb_b.shape), _resident(w.shape)],
        out_specs=[row(wd) for wd in widths],
        out_shape=[jax.ShapeDtypeStruct((rows, wd), dt) for wd, dt in zip(widths, dtypes)],
        compiler_params=_params(1),
        name="inproj",
    )(x, mod3, g_pre, cos, sin, lb_f, lb_b, w)


def _inproj_ctx_kernel(x_ref, mod_ref, gpre_ref, lbf_ref, lbb_ref, w_ref,
                       kv_ref, lff_ref, kff_ref, lfb_ref, kfb_ref, hv_ref, *, hw):
    x = x_ref[...]
    shift, scale = mod_ref[3:4, :], mod_ref[4:5, :]
    h = (_rms(x, gpre_ref[...]) * (1.0 + scale) + shift).astype(MXU_DTYPE)
    kvw = 2 * ATT_KV_WIDTH
    kv_ref[...] = jnp.dot(h, w_ref[:, :kvw], preferred_element_type=F32).astype(kv_ref.dtype)
    lf, kf = _forget(jnp.dot(h, w_ref[:, kvw:kvw + hw], preferred_element_type=F32), _lower_bound(lbf_ref))
    lff_ref[...] = lf
    kff_ref[...] = kf.astype(kff_ref.dtype)
    lf, kf = _forget(jnp.dot(h, w_ref[:, kvw + hw:kvw + 2 * hw], preferred_element_type=F32),
                     _lower_bound(lbb_ref))
    lfb_ref[...] = lf
    kfb_ref[...] = kf.astype(kfb_ref.dtype)
    hv_ref[...] = jnp.dot(h, w_ref[:, kvw + 2 * hw:], preferred_element_type=F32).astype(hv_ref.dtype)


def _inproj_ctx(xc, mod3, ctx_row, g_pre, lb_f, lb_b, w, tm):
    rows, d = xc.shape
    assert rows % tm == 0, (rows, tm)
    hw = lb_f.shape[1]
    row = lambda width: pl.BlockSpec((tm, width), lambda i: (i, 0))
    widths = [2 * ATT_KV_WIDTH, hw, hw, hw, hw, hw]
    dtypes = [MXU_DTYPE, F32, MXU_DTYPE, F32, MXU_DTYPE, MXU_DTYPE]
    return pl.pallas_call(
        functools.partial(_inproj_ctx_kernel, hw=hw),
        grid=(rows // tm,),
        in_specs=[row(d),
                  pl.BlockSpec((None, N_MOD, d), lambda i: (ctx_row, 0, 0)),
                  _resident((1, d)), _resident(lb_f.shape), _resident(lb_b.shape), _resident(w.shape)],
        out_specs=[row(wd) for wd in widths],
        out_shape=[jax.ShapeDtypeStruct((rows, wd), dt) for wd, dt in zip(widths, dtypes)],
        compiler_params=_params(1),
        name="inproj_ctx",
    )(xc, mod3, g_pre, lb_f, lb_b, w)


def _attn_kernel(sink_ref, q_ref, kv_ref, kvc_ref, o_ref, *, seq, band):
    step = pl.program_id(1)
    kvc = kvc_ref[...]
    low_lane = lax.broadcasted_iota(jnp.int32, (ATT_BLOCK, LANES), 1) < ATT_HEAD_DIM
    low_row = lax.broadcasted_iota(jnp.int32, (LANES, ATT_BLOCK), 0) < ATT_HEAD_DIM
    heads_per_group = 2 * ATT_SLABS

    def block_keys(qb):
        n = step * ATT_QBLOCKS + qb
        start = pl.multiple_of(jnp.clip(n * ATT_BLOCK - ATT_BLOCK, 0, seq - band), ATT_BLOCK)
        kvb = kv_ref[pl.ds(start, band), :]
        k_all = jnp.concatenate([kvb[:, :ATT_KV_WIDTH], kvc[:, :ATT_KV_WIDTH]], axis=0)
        v_t = jnp.concatenate([kvb[:, ATT_KV_WIDTH:], kvc[:, ATT_KV_WIDTH:]], axis=0).T
        kpos = start + lax.broadcasted_iota(jnp.int32, (band, ATT_BLOCK), 0)
        qpos = n * ATT_BLOCK + lax.broadcasted_iota(jnp.int32, (band, ATT_BLOCK), 1)
        valid = jnp.abs(kpos - qpos) <= WINDOW
        return k_all, v_t, jnp.concatenate([valid] * heads_per_group, axis=1)

    keys = [block_keys(qb) for qb in range(ATT_QBLOCKS)]

    def unit_scores(qb, m0):
        q_rows, sink_row = [], []
        for m in range(m0, m0 + ATT_SLABS):
            qm = q_ref[qb * ATT_BLOCK:(qb + 1) * ATT_BLOCK, m * LANES:(m + 1) * LANES]
            zero = jnp.zeros_like(qm)
            q_rows += [jnp.where(low_lane, qm, zero), jnp.where(low_lane, zero, qm)]
            sink_row += [jnp.full((1, ATT_BLOCK), sink_ref[m] * LOG2E, F32),
                         jnp.full((1, ATT_BLOCK), sink_ref[ATT_GROUP + m] * LOG2E, F32)]
        return _mm_nt(keys[qb][0], jnp.concatenate(q_rows, axis=0)), jnp.concatenate(sink_row, axis=1)

    units = [(qb, m0) for qb in range(ATT_QBLOCKS) for m0 in range(0, ATT_GROUP, ATT_SLABS)]
    pending = [unit_scores(*units[0])]
    for u, (qb, m0) in enumerate(units):
        if u + 1 < len(units):
            pending.append(unit_scores(*units[u + 1]))
        s_t, sink2 = pending[u]
        _, v_t, valid = keys[qb]
        s_t = jnp.concatenate([jnp.where(valid, s_t[:band], NEG_INF), s_t[band:]], axis=0)
        m = jnp.maximum(jnp.max(s_t, axis=0, keepdims=True), sink2)
        e = jnp.exp2(s_t - m)
        den = jnp.sum(e, axis=0, keepdims=True) + jnp.exp2(sink2 - m)
        o_t = _mm(v_t, e) * (1.0 / den)
        for j, m in enumerate(range(m0, m0 + ATT_SLABS)):
            o_lo = o_t[:, (2 * j) * ATT_BLOCK:(2 * j + 1) * ATT_BLOCK]
            o_hi = o_t[:, (2 * j + 1) * ATT_BLOCK:(2 * j + 2) * ATT_BLOCK]
            o_ref[qb * ATT_BLOCK:(qb + 1) * ATT_BLOCK, m * LANES:(m + 1) * LANES] = (
                jnp.where(low_row, o_lo, o_hi).T.astype(o_ref.dtype))


def _attention(sink, q, kv, kvc, batch):
    rows, _ = q.shape
    seq = rows // batch
    ctx_len = kvc.shape[0] // batch
    tq = ATT_QBLOCKS * ATT_BLOCK
    steps = seq // tq
    band = 3 * ATT_BLOCK
    return pl.pallas_call(
        functools.partial(_attn_kernel, seq=seq, band=band),
        grid=(batch, steps),
        in_specs=[pl.BlockSpec(memory_space=pltpu.SMEM),
                  pl.BlockSpec((tq, ATT_WIDTH), lambda b, n: (b * steps + n, 0)),
                  pl.BlockSpec((seq, 2 * ATT_KV_WIDTH), lambda b, n: (b, 0)),
                  pl.BlockSpec((ctx_len, 2 * ATT_KV_WIDTH), lambda b, n: (b, 0))],
        out_specs=pl.BlockSpec((tq, ATT_WIDTH), lambda b, n: (b * steps + n, 0)),
        out_shape=jax.ShapeDtypeStruct((rows, ATT_WIDTH), MXU_DTYPE),
        compiler_params=_params(2),
        name="attention",
    )(sink, q, kv, kvc)


def _cum_decays(lf_fwd, lf_bwd, tri):
    c = lf_fwd.shape[0]
    lf = jnp.concatenate([lf_fwd, lf_bwd], axis=1)
    hi = lf.astype(jnp.bfloat16)
    r1 = lf - hi.astype(F32)
    mid = r1.astype(jnp.bfloat16)
    lo = (r1 - mid.astype(F32)).astype(jnp.bfloat16)
    both = jnp.dot(tri, jnp.concatenate([hi, mid, lo], axis=0), preferred_element_type=F32)
    return [both[:c, :LANES], both[c:, LANES:]]


def _hg_state_update(st, k, v, b, b_end):
    k_dec = k * jnp.exp2(b_end - b)
    return st * jnp.exp2(b_end) + _mm_tn(v, k_dec)


def _hg_diag_tiles(q4, k4, b4, rev):
    nblk, sub, _ = q4.shape
    half = sub // 2
    tiles = []
    for s in range(sub):
        ks, bs = k4[:, s:s + 1, :], b4[:, s:s + 1, :]
        halves = []
        for hf in range(2):
            needed = (hf == 0 or s >= half) if rev else (hf == 1 or s < half)
            if needed:
                rows = slice(hf * half, (hf + 1) * half)
                halves.append(q4[:, rows] * ks * jnp.exp2(jnp.minimum(b4[:, rows] - bs, 0.0)))
            else:
                halves.append(jnp.zeros((nblk, half, LANES), F32))
        tiles.append(jnp.concatenate(halves, axis=1).reshape(nblk * sub, LANES).astype(MXU_DTYPE))
    return jnp.concatenate(tiles, axis=1)


def _hg_local(q, k, b, rev, exact):
    c, sub = HG_CHUNK, HG_SUB
    nblk = c // sub
    q4, k4, b4 = (a.reshape(nblk, sub, LANES) for a in (q, k, b))
    zero = jnp.zeros((sub, LANES), F32)

    def factors(q_blocks, k_blocks, anchor):
        ba = b[anchor:anchor + 1]
        qs = [q4[i] * jnp.exp2(b4[i] - ba) if i in q_blocks else zero for i in range(nblk)]
        ks = [k4[i] * jnp.exp2(ba - b4[i]) if i in k_blocks else zero for i in range(nblk)]
        return jnp.concatenate(qs, axis=0), jnp.concatenate(ks, axis=0)

    if rev:
        pairs = [factors((0,), (1,), sub), factors((2,), (3,), 3 * sub), factors((0, 1), (2, 3), 2 * sub)]
    else:
        pairs = [factors((1,), (0,), sub - 1), factors((3,), (2,), 3 * sub - 1), factors((2, 3), (0, 1), 2 * sub - 1)]
    if not exact:
        pairs += [factors((i,), (i,), i * sub + (sub - 1 if rev else 0)) for i in range(nblk)]
    qcat = jnp.concatenate([p[0] for p in pairs], axis=1).astype(MXU_DTYPE)
    kcat = jnp.concatenate([p[1] for p in pairs], axis=1).astype(MXU_DTYPE)
    return (_hg_diag_tiles(q4, k4, b4, rev) if exact else None), qcat, kcat


def _hgrn_kernel(q_ref, v_ref, lff_ref, kff_ref, lfb_ref, kfb_ref,
                 vc_ref, lffc_ref, kffc_ref, lfbc_ref, kfbc_ref, sel_ref, o_ref,
                 sf_ref, sb_ref, z_ref, qc_ref, kc_ref, b_ref):
    c, sub = HG_CHUNK, HG_SUB
    seq, ctx_len = q_ref.shape[0], vc_ref.shape[0]
    n, nc = seq // c, ctx_len // c
    r = lax.broadcasted_iota(jnp.int32, (c, c), 0)
    s = lax.broadcasted_iota(jnp.int32, (c, c), 1)
    tri = jnp.concatenate([jnp.concatenate([(s <= r).astype(jnp.bfloat16)] * 3, axis=1),
                           jnp.concatenate([(s >= r).astype(jnp.bfloat16)] * 3, axis=1)], axis=0)
    row = lax.broadcasted_iota(jnp.int32, (c, LANES), 0)
    lane = lax.broadcasted_iota(jnp.int32, (c, LANES), 1)
    own = (lane // sub) == (row // sub)
    dmask_f = own & ((lane % sub) <= (row % sub))
    dmask_b = own & ((lane % sub) >= (row % sub))
    pad = jnp.zeros((LANES - c, LANES), MXU_DTYPE)

    sf_ref[...] = jnp.zeros_like(sf_ref)
    sb_ref[...] = jnp.zeros_like(sb_ref)

    sf, sb = jnp.zeros(sf_ref.shape, F32), jnp.zeros(sb_ref.shape, F32)
    for i in range(nc):
        rf, rb = slice(i * c, (i + 1) * c), slice((nc - 1 - i) * c, (nc - i) * c)
        bf, bb = _cum_decays(lffc_ref[rf, :], lfbc_ref[rb, :], tri)
        sf = _hg_state_update(sf, kffc_ref[rf, :].astype(F32), vc_ref[rf, :], bf, bf[c - 1:c, :])
        sb = _hg_state_update(sb, kfbc_ref[rb, :].astype(F32), vc_ref[rb, :], bb, bb[0:1, :])
    sf_ref[...] = sf
    sb_ref[...] = sb

    dirs = ((kff_ref, sf_ref, dmask_f, False), (kfb_ref, sb_ref, dmask_b, True))

    def chunk_rows(i, rev):
        return pl.ds(pl.multiple_of(((n - 1 - i) if rev else i) * c, c), c)

    def cum_decays(i):
        return _cum_decays(lff_ref[chunk_rows(i, False), :], lfb_ref[chunk_rows(i, True), :], tri)

    def stage_local(i, bs, slot, exact):
        for j, (k_ref, _, _, rev) in enumerate(dirs):
            rows = chunk_rows(i, rev)
            zcat, qcat, kcat = _hg_local(q_ref[rows, :].astype(F32), k_ref[rows, :].astype(F32), bs[j], rev, exact)
            if exact:
                z_ref[slot, j * c:(j + 1) * c, :] = zcat
            qc_ref[slot, j, :, 0:qcat.shape[1]] = qcat
            kc_ref[slot, j, 0:c, 0:kcat.shape[1]] = kcat
            b_ref[slot, j] = bs[j]

    def step(i, slot, accumulate, exact):
        width = (N_OFF_SEGMENTS if exact else N_OFF_SEGMENTS + c // sub) * LANES
        nxt = jnp.minimum(i + 1, n - 1)
        b_next = cum_decays(nxt)
        if exact:
            diag = jnp.dot(z_ref[slot], sel_ref[...], preferred_element_type=F32)
        started = []
        for j, (k_ref, st_ref, _, rev) in enumerate(dirs):
            rows = chunk_rows(i, rev)
            q, k = q_ref[rows, :].astype(F32), k_ref[rows, :].astype(F32)
            v, b = v_ref[rows, :], b_ref[slot, j]
            st = st_ref[...]
            b_end = b[0:1, :] if rev else b[c - 1:c, :]
            factored = _mm_nt(qc_ref[slot, j, :, 0:width], kc_ref[slot, j, :, 0:width])
            rhs = jnp.concatenate([st.T.astype(MXU_DTYPE), v.astype(MXU_DTYPE), pad], axis=0)
            started.append((rows, (q * jnp.exp2(b)).astype(MXU_DTYPE), rhs, factored,
                            _hg_state_update(st, k, v, b, b_end)))
        stage_local(nxt, b_next, 1 - slot, exact)
        for j, (_, st_ref, dmask, rev) in enumerate(dirs):
            rows, q_dec, rhs, factored, st_new = started[j]
            if exact:
                scores = jnp.where(dmask, diag[j * c:(j + 1) * c], 0.0) + factored
            else:
                scores = jnp.where((lane >= row) if rev else (lane <= row), factored, 0.0)
            o = jnp.dot(jnp.concatenate([q_dec, scores.astype(MXU_DTYPE)], axis=1), rhs,
                        preferred_element_type=F32)
            st_ref[...] = st_new
            if accumulate:
                o_ref[rows, :] += o
            else:
                o_ref[rows, :] = o

    def scan(exact):
        stage_local(0, cum_decays(0), 0, exact)

        unroll = HG_UNROLL_EXACT if exact else HG_UNROLL

        def steps(ii, carry, accumulate):
            for u in range(unroll):
                step(ii * unroll + u, u % 2, accumulate, exact)
            return carry

        half_trips = n // (2 * unroll)
        lax.fori_loop(0, half_trips, functools.partial(steps, accumulate=False), 0)
        lax.fori_loop(half_trips, 2 * half_trips, functools.partial(steps, accumulate=True), 0)

    kc_ref[...] = jnp.zeros_like(kc_ref)
    steepest = jnp.minimum(jnp.min(lff_ref[...]), jnp.min(lfb_ref[...]))
    lax.cond(steepest * (sub - 1) >= -HG_SAFE_LOG2, lambda: scan(False), lambda: scan(True))


def _hgrn(hq, hv, lff, kff, lfb, kfb, hvc, lffc, kffc, lfbc, kfbc, batch):
    rows, hw = hq.shape
    seq = rows // batch
    ctx_len = hvc.shape[0] // batch
    heads = hw // HG_EXPAND
    assert seq % (2 * HG_UNROLL * HG_CHUNK) == 0 and ctx_len % HG_CHUNK == 0
    assert HG_UNROLL % HG_UNROLL_EXACT == 0 and HG_UNROLL_EXACT % 2 == 0
    n_seg = N_OFF_SEGMENTS + HG_CHUNK // HG_SUB
    src = jnp.arange(HG_SUB * LANES, dtype=jnp.int32)[:, None] // LANES
    sel = (src == jnp.arange(LANES, dtype=jnp.int32)[None, :] % HG_SUB).astype(MXU_DTYPE)
    lat = pl.BlockSpec((seq, HG_EXPAND), lambda b, h: (b, h))
    ctx = pl.BlockSpec((ctx_len, HG_EXPAND), lambda b, h: (b, h))
    return pl.pallas_call(
        _hgrn_kernel,
        grid=(batch, heads),
        in_specs=[lat] * 6 + [ctx] * 5 + [_resident(sel.shape)],
        out_specs=lat,
        out_shape=jax.ShapeDtypeStruct((rows, hw), F32),
        scratch_shapes=[pltpu.VMEM((HG_EXPAND, HG_EXPAND), F32)] * 2 + [
            pltpu.VMEM((2, 2 * HG_CHUNK, HG_SUB * LANES), MXU_DTYPE),
            pltpu.VMEM((2, 2, HG_CHUNK, n_seg * LANES), MXU_DTYPE),
            pltpu.VMEM((2, 2, LANES, n_seg * LANES), MXU_DTYPE),
            pltpu.VMEM((2, 2, HG_CHUNK, LANES), F32)],
        compiler_params=_params(2),
        name="hgrn",
    )(hq, hv, lff, kff, lfb, kfb, hvc, lffc, kffc, lfbc, kfbc, sel)


def _tail_kernel(x_ref, mod_ref, pre_ref, post_ref, ghn_ref, oa_ref, oh_ref,
                 wg_ref, woa_ref, woh_ref, wout_ref, win_ref, wo2_ref, o_ref, *, hw, d_ff):
    d = x_ref.shape[1]
    mod = lambda j: mod_ref[j:j + 1, :]
    bf = lambda a: a.astype(MXU_DTYPE)
    th = x_ref.shape[0] // FFN_SPLIT
    halves = [slice(i * th, (i + 1) * th) for i in range(FFN_SPLIT)]
    xs = [x_ref[r, :] for r in halves]
    gs = [_mm(bf(_rms(x, pre_ref[1:2, :]) * (1.0 + mod(4)) + mod(3)), wg_ref[...]) for x in xs]
    rs = [bf(_rms(oh_ref[r, :], ghn_ref[...]) * _silu(g[:, :hw])) for r, g in zip(halves, gs)]
    zs = [bf(_sigmoid(g[:, hw:hw + d]) * _mm(oa_ref[r, :], woa_ref[...])
             + _sigmoid(g[:, hw + d:]) * _mm(rr, woh_ref[...])) for r, g, rr in zip(halves, gs, rs)]
    xs = [x + mod(5) * _rms(_mm(z, wout_ref[...]), post_ref[1:2, :]) for x, z in zip(xs, zs)]
    hs = [bf(_rms(x, pre_ref[2:3, :]) * (1.0 + mod(7)) + mod(6)) for x in xs]
    gus = [_mm(h, win_ref[...]) for h in hs]
    acts = [bf(_silu(gu[:, :d_ff]) * gu[:, d_ff:]) for gu in gus]
    ys = [_mm(act, wo2_ref[...]) for act in acts]
    for r, x, y in zip(halves, xs, ys):
        o_ref[r, :] = x + 0.5 * mod(8) * _rms(y, post_ref[2:3, :])


def _tail(x, mod3, tiles_per_batch, pre, post, g_hnorm, oa, oh, wg, woa, woh, wout, w_in, w_out, tm):
    rows, d = x.shape
    assert rows % tm == 0, (rows, tm)
    row = lambda width: pl.BlockSpec((tm, width), lambda i: (i, 0))
    return pl.pallas_call(
        functools.partial(_tail_kernel, hw=g_hnorm.shape[1], d_ff=w_out.shape[0]),
        grid=(rows // tm,),
        in_specs=[row(d),
                  pl.BlockSpec((None, N_MOD, d), lambda i: (i // tiles_per_batch, 0, 0)),
                  _resident(pre.shape), _resident(post.shape), _resident(g_hnorm.shape),
                  row(oa.shape[1]), row(oh.shape[1])]
                 + [_resident(w.shape) for w in (wg, woa, woh, wout, w_in, w_out)],
        out_specs=row(d),
        out_shape=jax.ShapeDtypeStruct((rows, d), F32),
        compiler_params=_params(1),
        name="merge_ffn",
    )(x, mod3, pre, post, g_hnorm, oa, oh, wg, woa, woh, wout, w_in, w_out)


def _rope_tables(seq):
    quarter = ATT_HEAD_DIM // 4
    t = jnp.arange(seq, dtype=jnp.int32)
    freqs = ROPE_BASE ** (-jnp.arange(quarter, dtype=F32) / quarter)
    cos, sin = [], []
    for pos in (t // GRID_W, t % GRID_W):
        ang = pos.astype(F32)[:, None] * freqs[None, :]
        cos += [jnp.cos(ang), jnp.cos(ang)]
        sin += [-jnp.sin(ang), jnp.sin(ang)]
    reps = LANES // ATT_HEAD_DIM
    return jnp.concatenate(cos * reps, axis=1), jnp.concatenate(sin * reps, axis=1)


def _pair_heads(n_rows_or_cols):
    order = []
    for m in range(ATT_GROUP):
        for h in (m, ATT_GROUP + m):
            order += list(range(h * ATT_HEAD_DIM, (h + 1) * ATT_HEAD_DIM))
    assert len(order) == n_rows_or_cols
    return jnp.asarray(order, dtype=jnp.int32)


def kernel(x, c, ctx, c_ctx, w_ada, b_ada, norm_pre, norm_post, ffn1_w_in, ffn1_w_out, ffn2_w_in, ffn2_w_out,
           mix_w_in, attn_sink, hgrn_lb_fwd, hgrn_lb_bwd, hgrn_norm, w_o_attn, w_o_hgrn, w_out):
    batch, seq, d = x.shape
    ctx_len = ctx.shape[1]
    depth = w_ada.shape[0]
    assert depth == 1, "single-layer problem: the context stream is only read, never written back"
    hw = hgrn_norm.shape[1]
    tm = FFN_SPLIT * ROW_HALF
    tiles_per_batch = seq // tm
    bf = lambda a: a.astype(MXU_DTYPE)

    mod_rows = -(-(batch + 1) // 8) * 8
    cs = jnp.concatenate([c, c_ctx[None, :], jnp.zeros((mod_rows - batch - 1, d), F32)], axis=0)
    mod3 = _modulation(cs, w_ada[0], b_ada[0]).reshape(mod_rows, N_MOD, d)
    ctx_row = batch

    xl = x.reshape(batch * seq, d)
    xc = ctx.reshape(batch * ctx_len, d)
    pre = [norm_pre[0, j][None, :] for j in range(3)]
    post = [norm_post[0, j][None, :] for j in range(3)]

    w1_in, w1_out = bf(ffn1_w_in[0]), bf(ffn1_w_out[0])
    xl = _ffn(xl, mod3, lambda i: i // tiles_per_batch, 0, pre[0], post[0], w1_in, w1_out, tm)
    xc = _ffn(xc, mod3, lambda i: ctx_row, 0, pre[0], post[0], w1_in, w1_out, tm)

    wm = mix_w_in[0]
    kv0 = ATT_WIDTH
    hg0 = kv0 + 2 * ATT_KV_WIDTH
    gate0 = hg0 + 4 * hw
    w_lat = bf(jnp.concatenate([jnp.take(wm[:, :kv0], _pair_heads(ATT_WIDTH), axis=1), wm[:, kv0:gate0]], axis=1))
    w_ctx = bf(jnp.concatenate([wm[:, kv0:hg0], wm[:, hg0 + hw:gate0]], axis=1))
    cos, sin = _rope_tables(seq)
    q, kv, hq, lff, kff, lfb, kfb, hv = _inproj(
        xl, mod3, tiles_per_batch, pre[1], cos, sin, hgrn_lb_fwd, hgrn_lb_bwd, w_lat, tm)
    kvc, lffc, kffc, lfbc, kfbc, hvc = _inproj_ctx(xc, mod3, ctx_row, pre[1], hgrn_lb_fwd, hgrn_lb_bwd, w_ctx,
                                                   ROW_HALF)

    o_att = _attention(attn_sink[0], q, kv, kvc, batch)
    o_hg = _hgrn(hq, hv, lff, kff, lfb, kfb, hvc, lffc, kffc, lfbc, kfbc, batch)

    woa = bf(jnp.take(w_o_attn[0], _pair_heads(ATT_WIDTH), axis=0))
    xl = _tail(xl, mod3, tiles_per_batch, norm_pre[0], norm_post[0], hgrn_norm, o_att, o_hg,
               bf(wm[:, gate0:]), woa, bf(w_o_hgrn[0]), bf(w_out[0]), bf(ffn2_w_in[0]), bf(ffn2_w_out[0]), tm)
    return xl.reshape(batch, seq, d)
```

```python
import functools

import jax
import jax.numpy as jnp
import numpy as np
from jax import lax
from jax.experimental import pallas as pl
from jax.experimental.pallas import tpu as pltpu

F32 = jnp.float32
MXU_DTYPE = jnp.bfloat16

N_MOD = 9
GRID_W = 64
ATT_HEADS = 8
ATT_KV_HEADS = 2
ATT_GROUP = ATT_HEADS // ATT_KV_HEADS
ATT_HEAD_DIM = 64
ATT_WIDTH = ATT_HEADS * ATT_HEAD_DIM
ATT_KV_WIDTH = ATT_KV_HEADS * ATT_HEAD_DIM
WINDOW = 128
ATT_BLOCK = 128
ATT_SLABS = 1
ATT_QBLOCKS = 8
ATT_LOOKAHEAD = 3
ROPE_BASE = 10000.0
HG_EXPAND = 128
FFN_SPLIT = 2
ROW_HALF = 256
EPS = 1e-6
NEG_INF = -1e30
LOG2E = 1.4426950408889634

LANES = 128
HG_CHUNK = 64
HG_SUB = 16
HG_UNROLL = 8
HG_UNROLL_EXACT = 2
N_OFF_SEGMENTS = 3
HG_SAFE_LOG2 = 120.0
VMEM_LIMIT = 56 * 1024 * 1024


def _sigmoid(x):
    return 1.0 / (1.0 + jnp.exp(-x))


def _silu(x):
    return x * _sigmoid(x)


def _rms(x, g):
    ms = jnp.mean(x * x, axis=-1, keepdims=True)
    return x * lax.rsqrt(ms + EPS) * g


def _mm(a, b):
    return jnp.dot(a.astype(MXU_DTYPE), b.astype(MXU_DTYPE), preferred_element_type=F32)


def _mm_nt(a, b):
    return lax.dot_general(a.astype(MXU_DTYPE), b.astype(MXU_DTYPE), (((1,), (1,)), ((), ())),
                           preferred_element_type=F32)


def _mm_tn(a, b):
    return lax.dot_general(a.astype(MXU_DTYPE), b.astype(MXU_DTYPE), (((0,), (0,)), ((), ())),
                           preferred_element_type=F32)


def _resident(shape):
    nd = len(shape)
    return pl.BlockSpec(shape, lambda *_: (0,) * nd, pipeline_mode=pl.Buffered(1))


def _params(n_axes):
    return pltpu.CompilerParams(dimension_semantics=("arbitrary",) * n_axes,
                                vmem_limit_bytes=VMEM_LIMIT)


def _mod_kernel(c_ref, w_ref, b_ref, o_ref):
    o_ref[...] = _mm(_silu(c_ref[...]), w_ref[...]) + b_ref[...]


def _modulation(cs, w_ada, b_ada):
    rows, d = cs.shape
    n = w_ada.shape[1]
    tn = n // 8
    return pl.pallas_call(
        _mod_kernel,
        grid=(n // tn,),
        in_specs=[pl.BlockSpec((rows, d), lambda j: (0, 0)),
                  pl.BlockSpec((d, tn), lambda j: (0, j)),
                  pl.BlockSpec((1, tn), lambda j: (0, j))],
        out_specs=pl.BlockSpec((rows, tn), lambda j: (0, j)),
        out_shape=jax.ShapeDtypeStruct((rows, n), F32),
        compiler_params=_params(1),
        name="modulation",
    )(cs, w_ada, b_ada.reshape(1, n))


def _ffn_kernel(x_ref, mod_ref, gpre_ref, gpost_ref, win_ref, wout_ref, o_ref, *, j0, d_ff):
    shift, scale, gate = mod_ref[j0:j0 + 1, :], mod_ref[j0 + 1:j0 + 2, :], mod_ref[j0 + 2:j0 + 3, :]
    th = x_ref.shape[0] // FFN_SPLIT
    halves = [slice(i * th, (i + 1) * th) for i in range(FFN_SPLIT)]
    xs = [x_ref[r, :] for r in halves]
    hs = [(_rms(x, gpre_ref[...]) * (1.0 + scale) + shift).astype(MXU_DTYPE) for x in xs]
    gus = [_mm(h, win_ref[...]) for h in hs]
    acts = [(_silu(gu[:, :d_ff]) * gu[:, d_ff:]).astype(MXU_DTYPE) for gu in gus]
    ys = [_mm(act, wout_ref[...]) for act in acts]
    for r, x, y in zip(halves, xs, ys):
        o_ref[r, :] = x + 0.5 * gate * _rms(y, gpost_ref[...])


def _ffn(x, mod3, mod_row, j0, g_pre, g_post, w_in, w_out, tm):
    rows, d = x.shape
    assert rows % tm == 0, (rows, tm)
    d_ff = w_out.shape[0]
    return pl.pallas_call(
        functools.partial(_ffn_kernel, j0=j0, d_ff=d_ff),
        grid=(rows // tm,),
        in_specs=[pl.BlockSpec((tm, d), lambda i: (i, 0)),
                  pl.BlockSpec((None, N_MOD, d), lambda i: (mod_row(i), 0, 0)),
                  _resident((1, d)), _resident((1, d)),
                  _resident(w_in.shape), _resident(w_out.shape)],
        out_specs=pl.BlockSpec((tm, d), lambda i: (i, 0)),
        out_shape=jax.ShapeDtypeStruct((rows, d), F32),
        compiler_params=_params(1),
        name="ffn",
    )(x, mod3, g_pre, g_post, w_in, w_out)


def _lower_bound(lb_ref):
    a = lb_ref[...]
    e = jnp.exp(a - jnp.max(a, axis=0, keepdims=True))
    return e[0:1, :] / jnp.sum(e, axis=0, keepdims=True)


def _forget(z, lb):
    f = lb + (1.0 - lb) * _sigmoid(z)
    return jnp.log2(f), 1.0 - f


def _rope(x, cos, sin):
    w = x.shape[1]
    reps = w // LANES
    cos_w = jnp.concatenate([cos] * reps, axis=1) if reps > 1 else cos
    sin_w = jnp.concatenate([sin] * reps, axis=1) if reps > 1 else sin
    lane = lax.broadcasted_iota(jnp.int32, x.shape, 1)
    quarter = ATT_HEAD_DIM // 4
    first = (lane % (2 * quarter)) < quarter
    partner = jnp.where(first, pltpu.roll(x, w - quarter, 1), pltpu.roll(x, quarter, 1))
    return x * cos_w + partner * sin_w


def _inproj_kernel(x_ref, mod_ref, gpre_ref, cos_ref, sin_ref, lbf_ref, lbb_ref, w_ref,
                   q_ref, kv_ref, hq_ref, lff_ref, kff_ref, lfb_ref, kfb_ref, hv_ref, *, hw):
    shift, scale = mod_ref[3:4, :], mod_ref[4:5, :]
    th = x_ref.shape[0] // FFN_SPLIT
    halves = [slice(i * th, (i + 1) * th) for i in range(FFN_SPLIT)]
    hs = [(_rms(x_ref[r, :], gpre_ref[...]) * (1.0 + scale) + shift).astype(MXU_DTYPE) for r in halves]
    c0 = 0

    def proj(width):
        nonlocal c0
        outs = [jnp.dot(h, w_ref[:, c0:c0 + width], preferred_element_type=F32) for h in hs]
        c0 += width
        return zip(halves, outs)

    for r, q in proj(ATT_WIDTH):
        q_ref[r, :] = (_rope(q, cos_ref[r, :], sin_ref[r, :]) * (LOG2E * ATT_HEAD_DIM ** -0.5)).astype(q_ref.dtype)
    for r, kv in proj(2 * ATT_KV_WIDTH):
        kv_ref[r, :ATT_KV_WIDTH] = _rope(kv[:, :ATT_KV_WIDTH], cos_ref[r, :], sin_ref[r, :]).astype(kv_ref.dtype)
        kv_ref[r, ATT_KV_WIDTH:] = kv[:, ATT_KV_WIDTH:].astype(kv_ref.dtype)
    for r, z in proj(hw):
        hq_ref[r, :] = _silu(z).astype(hq_ref.dtype)
    for lb_ref, lf_ref, kf_ref in ((lbf_ref, lff_ref, kff_ref), (lbb_ref, lfb_ref, kfb_ref)):
        for r, z in proj(hw):
            lf, kf = _forget(z, _lower_bound(lb_ref))
            lf_ref[r, :] = lf
            kf_ref[r, :] = kf.astype(kf_ref.dtype)
    for r, z in proj(hw):
        hv_ref[r, :] = z.astype(hv_ref.dtype)


def _inproj(x, mod3, tiles_per_batch, g_pre, cos, sin, lb_f, lb_b, w, tm):
    rows, d = x.shape
    assert rows % tm == 0, (rows, tm)
    hw = lb_f.shape[1]
    row = lambda width: pl.BlockSpec((tm, width), lambda i: (i, 0))
    widths = [ATT_WIDTH, 2 * ATT_KV_WIDTH, hw, hw, hw, hw, hw, hw]
    dtypes = [MXU_DTYPE, MXU_DTYPE, MXU_DTYPE, F32, MXU_DTYPE, F32, MXU_DTYPE, MXU_DTYPE]
    return pl.pallas_call(
        functools.partial(_inproj_kernel, hw=hw),
        grid=(rows // tm,),
        in_specs=[row(d),
                  pl.BlockSpec((None, N_MOD, d), lambda i: (i // tiles_per_batch, 0, 0)),
                  _resident((1, d)),
                  pl.BlockSpec((tm, LANES), lambda i: (i % tiles_per_batch, 0)),
                  pl.BlockSpec((tm, LANES), lambda i: (i % tiles_per_batch, 0)),
                  _resident(lb_f.shape), _resident(lb_b.shape), _resident(w.shape)],
        out_specs=[row(wd) for wd in widths],
        out_shape=[jax.ShapeDtypeStruct((rows, wd), dt) for wd, dt in zip(widths, dtypes)],
        compiler_params=_params(1),
        name="inproj",
    )(x, mod3, g_pre, cos, sin, lb_f, lb_b, w)


def _inproj_ctx_kernel(x_ref, mod_ref, gpre_ref, lbf_ref, lbb_ref, w_ref,
                       kv_ref, lff_ref, kff_ref, lfb_ref, kfb_ref, hv_ref, *, hw):
    x = x_ref[...]
    shift, scale = mod_ref[3:4, :], mod_ref[4:5, :]
    h = (_rms(x, gpre_ref[...]) * (1.0 + scale) + shift).astype(MXU_DTYPE)
    kvw = 2 * ATT_KV_WIDTH
    kv_ref[...] = jnp.dot(h, w_ref[:, :kvw], preferred_element_type=F32).astype(kv_ref.dtype)
    lf, kf = _forget(jnp.dot(h, w_ref[:, kvw:kvw + hw], preferred_element_type=F32), _lower_bound(lbf_ref))
    lff_ref[...] = lf
    kff_ref[...] = kf.astype(kff_ref.dtype)
    lf, kf = _forget(jnp.dot(h, w_ref[:, kvw + hw:kvw + 2 * hw], preferred_element_type=F32),
                     _lower_bound(lbb_ref))
    lfb_ref[...] = lf
    kfb_ref[...] = kf.astype(kfb_ref.dtype)
    hv_ref[...] = jnp.dot(h, w_ref[:, kvw + 2 * hw:], preferred_element_type=F32).astype(hv_ref.dtype)


def _inproj_ctx(xc, mod3, ctx_row, g_pre, lb_f, lb_b, w, tm):
    rows, d = xc.shape
    assert rows % tm == 0, (rows, tm)
    hw = lb_f.shape[1]
    row = lambda width: pl.BlockSpec((tm, width), lambda i: (i, 0))
    widths = [2 * ATT_KV_WIDTH, hw, hw, hw, hw, hw]
    dtypes = [MXU_DTYPE, F32, MXU_DTYPE, F32, MXU_DTYPE, MXU_DTYPE]
    return pl.pallas_call(
        functools.partial(_inproj_ctx_kernel, hw=hw),
        grid=(rows // tm,),
        in_specs=[row(d),
                  pl.BlockSpec((None, N_MOD, d), lambda i: (ctx_row, 0, 0)),
                  _resident((1, d)), _resident(lb_f.shape), _resident(lb_b.shape), _resident(w.shape)],
        out_specs=[row(wd) for wd in widths],
        out_shape=[jax.ShapeDtypeStruct((rows, wd), dt) for wd, dt in zip(widths, dtypes)],
        compiler_params=_params(1),
        name="inproj_ctx",
    )(xc, mod3, g_pre, lb_f, lb_b, w)


def _attn_kernel(sink_ref, q_ref, kv_ref, kvc_ref, o_ref, *, seq, band):
    step = pl.program_id(1)
    kvc = kvc_ref[...]
    low_lane = lax.broadcasted_iota(jnp.int32, (ATT_BLOCK, LANES), 1) < ATT_HEAD_DIM
    low_row = lax.broadcasted_iota(jnp.int32, (LANES, ATT_BLOCK), 0) < ATT_HEAD_DIM
    heads_per_group = 2 * ATT_SLABS

    def block_keys(qb):
        n = step * ATT_QBLOCKS + qb
        start = pl.multiple_of(jnp.clip(n * ATT_BLOCK - ATT_BLOCK, 0, seq - band), ATT_BLOCK)
        kvb = kv_ref[pl.ds(start, band), :]
        k_all = jnp.concatenate([kvb[:, :ATT_KV_WIDTH], kvc[:, :ATT_KV_WIDTH]], axis=0)
        v_t = jnp.concatenate([kvb[:, ATT_KV_WIDTH:], kvc[:, ATT_KV_WIDTH:]], axis=0).T
        kpos = start + lax.broadcasted_iota(jnp.int32, (band, ATT_BLOCK), 0)
        qpos = n * ATT_BLOCK + lax.broadcasted_iota(jnp.int32, (band, ATT_BLOCK), 1)
        valid = jnp.abs(kpos - qpos) <= WINDOW
        return k_all, v_t, jnp.concatenate([valid] * heads_per_group, axis=1)

    keys = [block_keys(qb) for qb in range(ATT_QBLOCKS)]

    def unit_scores(qb, m0):
        q_rows, sink_row = [], []
        for m in range(m0, m0 + ATT_SLABS):
            qm = q_ref[qb * ATT_BLOCK:(qb + 1) * ATT_BLOCK, m * LANES:(m + 1) * LANES]
            zero = jnp.zeros_like(qm)
            q_rows += [jnp.where(low_lane, qm, zero), jnp.where(low_lane, zero, qm)]
            sink_row += [jnp.full((1, ATT_BLOCK), sink_ref[m] * LOG2E, F32),
                         jnp.full((1, ATT_BLOCK), sink_ref[ATT_GROUP + m] * LOG2E, F32)]
        return _mm_nt(keys[qb][0], jnp.concatenate(q_rows, axis=0)), jnp.concatenate(sink_row, axis=1)

    units = [(qb, m0) for qb in range(ATT_QBLOCKS) for m0 in range(0, ATT_GROUP, ATT_SLABS)]

    def finish(o_t, den, qb, m0):
        o_t = o_t * (1.0 / den)
        for j, m in enumerate(range(m0, m0 + ATT_SLABS)):
            o_lo = o_t[:, (2 * j) * ATT_BLOCK:(2 * j + 1) * ATT_BLOCK]
            o_hi = o_t[:, (2 * j + 1) * ATT_BLOCK:(2 * j + 2) * ATT_BLOCK]
            o_ref[qb * ATT_BLOCK:(qb + 1) * ATT_BLOCK, m * LANES:(m + 1) * LANES] = (
                jnp.where(low_row, o_lo, o_hi).T.astype(o_ref.dtype))

    pending = [unit_scores(*unit) for unit in units[:ATT_LOOKAHEAD]]
    unfinished = None
    for u, (qb, m0) in enumerate(units):
        if u + ATT_LOOKAHEAD < len(units):
            pending.append(unit_scores(*units[u + ATT_LOOKAHEAD]))
        s_t, sink2 = pending[u]
        _, v_t, valid = keys[qb]
        s_t = jnp.concatenate([jnp.where(valid, s_t[:band], NEG_INF), s_t[band:]], axis=0)
        m = jnp.maximum(jnp.max(s_t, axis=0, keepdims=True), sink2)
        e = jnp.exp2(s_t - m)
        den = jnp.sum(e, axis=0, keepdims=True) + jnp.exp2(sink2 - m)
        o_t = _mm(v_t, e)
        if unfinished is not None:
            finish(*unfinished)
        unfinished = (o_t, den, qb, m0)
    finish(*unfinished)


def _attention(sink, q, kv, kvc, batch):
    rows, _ = q.shape
    seq = rows // batch
    ctx_len = kvc.shape[0] // batch
    tq = ATT_QBLOCKS * ATT_BLOCK
    steps = seq // tq
    band = 3 * ATT_BLOCK
    return pl.pallas_call(
        functools.partial(_attn_kernel, seq=seq, band=band),
        grid=(batch, steps),
        in_specs=[pl.BlockSpec(memory_space=pltpu.SMEM),
                  pl.BlockSpec((tq, ATT_WIDTH), lambda b, n: (b * steps + n, 0)),
                  pl.BlockSpec((seq, 2 * ATT_KV_WIDTH), lambda b, n: (b, 0)),
                  pl.BlockSpec((ctx_len, 2 * ATT_KV_WIDTH), lambda b, n: (b, 0))],
        out_specs=pl.BlockSpec((tq, ATT_WIDTH), lambda b, n: (b * steps + n, 0)),
        out_shape=jax.ShapeDtypeStruct((rows, ATT_WIDTH), MXU_DTYPE),
        compiler_params=_params(2),
        name="attention",
    )(sink, q, kv, kvc)


def _cum_decays(lf_fwd, lf_bwd, tri):
    c = lf_fwd.shape[0]
    lf = jnp.concatenate([lf_fwd, lf_bwd], axis=1)
    hi = lf.astype(jnp.bfloat16)
    r1 = lf - hi.astype(F32)
    mid = r1.astype(jnp.bfloat16)
    lo = (r1 - mid.astype(F32)).astype(jnp.bfloat16)
    both = jnp.dot(tri, jnp.concatenate([hi, mid, lo], axis=0), preferred_element_type=F32)
    return [both[:c, :LANES], both[c:, LANES:]]


def _hg_state_update(st, k, v, b, b_end):
    k_dec = k * jnp.exp2(b_end - b)
    return st * jnp.exp2(b_end) + _mm_tn(v, k_dec)


def _hg_diag_tiles(q4, k4, b4, rev):
    nblk, sub, _ = q4.shape
    half = sub // 2
    tiles = []
    for s in range(sub):
        ks, bs = k4[:, s:s + 1, :], b4[:, s:s + 1, :]
        halves = []
        for hf in range(2):
            needed = (hf == 0 or s >= half) if rev else (hf == 1 or s < half)
            if needed:
                rows = slice(hf * half, (hf + 1) * half)
                halves.append(q4[:, rows] * ks * jnp.exp2(jnp.minimum(b4[:, rows] - bs, 0.0)))
            else:
                halves.append(jnp.zeros((nblk, half, LANES), F32))
        tiles.append(jnp.concatenate(halves, axis=1).reshape(nblk * sub, LANES).astype(MXU_DTYPE))
    return jnp.concatenate(tiles, axis=1)


def _hg_local(q, k, b, rev, exact):
    c, sub = HG_CHUNK, HG_SUB
    nblk = c // sub
    q4, k4, b4 = (a.reshape(nblk, sub, LANES) for a in (q, k, b))
    zero = jnp.zeros((sub, LANES), F32)

    def factors(q_blocks, k_blocks, anchor):
        ba = b[anchor:anchor + 1]
        qs = [q4[i] * jnp.exp2(b4[i] - ba) if i in q_blocks else zero for i in range(nblk)]
        ks = [k4[i] * jnp.exp2(ba - b4[i]) if i in k_blocks else zero for i in range(nblk)]
        return jnp.concatenate(qs, axis=0), jnp.concatenate(ks, axis=0)

    if rev:
        pairs = [factors((0,), (1,), sub), factors((2,), (3,), 3 * sub), factors((0, 1), (2, 3), 2 * sub)]
    else:
        pairs = [factors((1,), (0,), sub - 1), factors((3,), (2,), 3 * sub - 1), factors((2, 3), (0, 1), 2 * sub - 1)]
    if not exact:
        anchors = [b4[i, (sub - 1 if rev else 0):(sub if rev else 1), :] for i in range(nblk)]
        pairs.append((jnp.concatenate([q4[i] * jnp.exp2(b4[i] - anchors[i]) for i in range(nblk)], axis=0),
                      jnp.concatenate([k4[i] * jnp.exp2(anchors[i] - b4[i]) for i in range(nblk)], axis=0)))
    qcat = jnp.concatenate([p[0] for p in pairs], axis=1).astype(MXU_DTYPE)
    kcat = jnp.concatenate([p[1] for p in pairs], axis=1).astype(MXU_DTYPE)
    return (_hg_diag_tiles(q4, k4, b4, rev) if exact else None), qcat, kcat


def _hgrn_kernel(q_ref, v_ref, lff_ref, kff_ref, lfb_ref, kfb_ref,
                 vc_ref, lffc_ref, kffc_ref, lfbc_ref, kfbc_ref, sel_ref, o_ref,
                 sf_ref, sb_ref, z_ref, qc_ref, kc_ref, b_ref):
    c, sub = HG_CHUNK, HG_SUB
    seq, ctx_len = q_ref.shape[0], vc_ref.shape[0]
    n, nc = seq // c, ctx_len // c
    r = lax.broadcasted_iota(jnp.int32, (c, c), 0)
    s = lax.broadcasted_iota(jnp.int32, (c, c), 1)
    tri = jnp.concatenate([jnp.concatenate([(s <= r).astype(jnp.bfloat16)] * 3, axis=1),
                           jnp.concatenate([(s >= r).astype(jnp.bfloat16)] * 3, axis=1)], axis=0)
    row = lax.broadcasted_iota(jnp.int32, (c, LANES), 0)
    lane = lax.broadcasted_iota(jnp.int32, (c, LANES), 1)
    own = (lane // sub) == (row // sub)
    dmask_f = own & ((lane % sub) <= (row % sub))
    dmask_b = own & ((lane % sub) >= (row % sub))
    pad = jnp.zeros((LANES - c, LANES), MXU_DTYPE)

    sf_ref[...] = jnp.zeros_like(sf_ref)
    sb_ref[...] = jnp.zeros_like(sb_ref)

    sf, sb = jnp.zeros(sf_ref.shape, F32), jnp.zeros(sb_ref.shape, F32)
    for i in range(nc):
        rf, rb = slice(i * c, (i + 1) * c), slice((nc - 1 - i) * c, (nc - i) * c)
        bf, bb = _cum_decays(lffc_ref[rf, :], lfbc_ref[rb, :], tri)
        sf = _hg_state_update(sf, kffc_ref[rf, :].astype(F32), vc_ref[rf, :], bf, bf[c - 1:c, :])
        sb = _hg_state_update(sb, kfbc_ref[rb, :].astype(F32), vc_ref[rb, :], bb, bb[0:1, :])
    sf_ref[...] = sf
    sb_ref[...] = sb

    dirs = ((kff_ref, sf_ref, dmask_f, False), (kfb_ref, sb_ref, dmask_b, True))

    def chunk_rows(i, rev):
        return pl.ds(pl.multiple_of(((n - 1 - i) if rev else i) * c, c), c)

    def cum_decays(i):
        return _cum_decays(lff_ref[chunk_rows(i, False), :], lfb_ref[chunk_rows(i, True), :], tri)

    def stage_local(i, bs, slot, exact):
        for j, (k_ref, _, _, rev) in enumerate(dirs):
            rows = chunk_rows(i, rev)
            zcat, qcat, kcat = _hg_local(q_ref[rows, :].astype(F32), k_ref[rows, :].astype(F32), bs[j], rev, exact)
            if exact:
                z_ref[slot, j * c:(j + 1) * c, :] = zcat
            qc_ref[slot, j, :, 0:qcat.shape[1]] = qcat
            kc_ref[slot, j, 0:c, 0:kcat.shape[1]] = kcat
            b_ref[slot, j] = bs[j]

    def step(i, slot, accumulate, exact):
        off = N_OFF_SEGMENTS * LANES
        nxt = jnp.minimum(i + 1, n - 1)
        b_next = cum_decays(nxt)
        if exact:
            diag_both = jnp.dot(z_ref[slot], sel_ref[...], preferred_element_type=F32)
        started = []
        for j, (k_ref, st_ref, _, rev) in enumerate(dirs):
            rows = chunk_rows(i, rev)
            q, k = q_ref[rows, :].astype(F32), k_ref[rows, :].astype(F32)
            v, b = v_ref[rows, :], b_ref[slot, j]
            st = st_ref[...]
            b_end = b[0:1, :] if rev else b[c - 1:c, :]
            below = _mm_nt(qc_ref[slot, j, :, 0:off], kc_ref[slot, j, :, 0:off])
            if exact:
                diag = diag_both[j * c:(j + 1) * c]
            else:
                diag = _mm_nt(qc_ref[slot, j, :, off:off + LANES], kc_ref[slot, j, :, off:off + LANES])
            rhs = jnp.concatenate([st.T.astype(MXU_DTYPE), v.astype(MXU_DTYPE), pad], axis=0)
            started.append((rows, (q * jnp.exp2(b)).astype(MXU_DTYPE), rhs, below, diag,
                            _hg_state_update(st, k, v, b, b_end)))
        stage_local(nxt, b_next, 1 - slot, exact)
        for j, (_, st_ref, dmask, rev) in enumerate(dirs):
            rows, q_dec, rhs, below, diag, st_new = started[j]
            scores = jnp.where(dmask, diag, 0.0) + below
            o = jnp.dot(jnp.concatenate([q_dec, scores.astype(MXU_DTYPE)], axis=1), rhs,
                        preferred_element_type=F32)
            st_ref[...] = st_new
            if accumulate:
                o_ref[rows, :] += o
            else:
                o_ref[rows, :] = o

    def scan(exact):
        stage_local(0, cum_decays(0), 0, exact)

        unroll = HG_UNROLL_EXACT if exact else HG_UNROLL

        def steps(ii, carry, accumulate):
            for u in range(unroll):
                step(ii * unroll + u, u % 2, accumulate, exact)
            return carry

        half_trips = n // (2 * unroll)
        lax.fori_loop(0, half_trips, functools.partial(steps, accumulate=False), 0)
        lax.fori_loop(half_trips, 2 * half_trips, functools.partial(steps, accumulate=True), 0)

    kc_ref[...] = jnp.zeros_like(kc_ref)
    steepest = jnp.minimum(jnp.min(lff_ref[...]), jnp.min(lfb_ref[...]))
    lax.cond(steepest * (sub - 1) >= -HG_SAFE_LOG2, lambda: scan(False), lambda: scan(True))


def _hgrn(hq, hv, lff, kff, lfb, kfb, hvc, lffc, kffc, lfbc, kfbc, batch):
    rows, hw = hq.shape
    seq = rows // batch
    ctx_len = hvc.shape[0] // batch
    heads = hw // HG_EXPAND
    assert seq % (2 * HG_UNROLL * HG_CHUNK) == 0 and ctx_len % HG_CHUNK == 0
    assert HG_UNROLL % HG_UNROLL_EXACT == 0 and HG_UNROLL_EXACT % 2 == 0
    n_seg = N_OFF_SEGMENTS + 1
    src = np.arange(HG_SUB * LANES, dtype=np.int32)[:, None] // LANES
    sel = jnp.asarray(src == np.arange(LANES, dtype=np.int32)[None, :] % HG_SUB, MXU_DTYPE)
    lat = pl.BlockSpec((seq, HG_EXPAND), lambda b, h: (b, h))
    ctx = pl.BlockSpec((ctx_len, HG_EXPAND), lambda b, h: (b, h))
    return pl.pallas_call(
        _hgrn_kernel,
        grid=(batch, heads),
        in_specs=[lat] * 6 + [ctx] * 5 + [_resident(sel.shape)],
        out_specs=lat,
        out_shape=jax.ShapeDtypeStruct((rows, hw), F32),
        scratch_shapes=[pltpu.VMEM((HG_EXPAND, HG_EXPAND), F32)] * 2 + [
            pltpu.VMEM((2, 2 * HG_CHUNK, HG_SUB * LANES), MXU_DTYPE),
            pltpu.VMEM((2, 2, HG_CHUNK, n_seg * LANES), MXU_DTYPE),
            pltpu.VMEM((2, 2, LANES, n_seg * LANES), MXU_DTYPE),
            pltpu.VMEM((2, 2, HG_CHUNK, LANES), F32)],
        compiler_params=_params(2),
        name="hgrn",
    )(hq, hv, lff, kff, lfb, kfb, hvc, lffc, kffc, lfbc, kfbc, sel)


def _tail_kernel(x_ref, mod_ref, pre_ref, post_ref, ghn_ref, oa_ref, oh_ref,
                 wg_ref, woa_ref, woh_ref, wout_ref, win_ref, wo2_ref, o_ref, *, hw, d_ff):
    d = x_ref.shape[1]
    mod = lambda j: mod_ref[j:j + 1, :]
    bf = lambda a: a.astype(MXU_DTYPE)
    th = x_ref.shape[0] // FFN_SPLIT
    halves = [slice(i * th, (i + 1) * th) for i in range(FFN_SPLIT)]
    xs = [x_ref[r, :] for r in halves]
    gs = [_mm(bf(_rms(x, pre_ref[1:2, :]) * (1.0 + mod(4)) + mod(3)), wg_ref[...]) for x in xs]
    rs = [bf(_rms(oh_ref[r, :], ghn_ref[...]) * _silu(g[:, :hw])) for r, g in zip(halves, gs)]
    zs = [bf(_sigmoid(g[:, hw:hw + d]) * _mm(oa_ref[r, :], woa_ref[...])
             + _sigmoid(g[:, hw + d:]) * _mm(rr, woh_ref[...])) for r, g, rr in zip(halves, gs, rs)]
    xs = [x + mod(5) * _rms(_mm(z, wout_ref[...]), post_ref[1:2, :]) for x, z in zip(xs, zs)]
    hs = [bf(_rms(x, pre_ref[2:3, :]) * (1.0 + mod(7)) + mod(6)) for x in xs]
    gus = [_mm(h, win_ref[...]) for h in hs]
    acts = [bf(_silu(gu[:, :d_ff]) * gu[:, d_ff:]) for gu in gus]
    ys = [_mm(act, wo2_ref[...]) for act in acts]
    for r, x, y in zip(halves, xs, ys):
        o_ref[r, :] = x + 0.5 * mod(8) * _rms(y, post_ref[2:3, :])


def _tail(x, mod3, tiles_per_batch, pre, post, g_hnorm, oa, oh, wg, woa, woh, wout, w_in, w_out, tm):
    rows, d = x.shape
    assert rows % tm == 0, (rows, tm)
    row = lambda width: pl.BlockSpec((tm, width), lambda i: (i, 0))
    return pl.pallas_call(
        functools.partial(_tail_kernel, hw=g_hnorm.shape[1], d_ff=w_out.shape[0]),
        grid=(rows // tm,),
        in_specs=[row(d),
                  pl.BlockSpec((None, N_MOD, d), lambda i: (i // tiles_per_batch, 0, 0)),
                  _resident(pre.shape), _resident(post.shape), _resident(g_hnorm.shape),
                  row(oa.shape[1]), row(oh.shape[1])]
                 + [_resident(w.shape) for w in (wg, woa, woh, wout, w_in, w_out)],
        out_specs=row(d),
        out_shape=jax.ShapeDtypeStruct((rows, d), F32),
        compiler_params=_params(1),
        name="merge_ffn",
    )(x, mod3, pre, post, g_hnorm, oa, oh, wg, woa, woh, wout, w_in, w_out)


def _rope_tables(seq):
    quarter = ATT_HEAD_DIM // 4
    t = np.arange(seq, dtype=np.int32)
    freqs = np.float32(ROPE_BASE) ** (-np.arange(quarter, dtype=np.float32) / np.float32(quarter))
    cos, sin = [], []
    for pos in (t // GRID_W, t % GRID_W):
        ang = pos.astype(np.float32)[:, None] * freqs[None, :]
        cos += [np.cos(ang), np.cos(ang)]
        sin += [-np.sin(ang), np.sin(ang)]
    reps = LANES // ATT_HEAD_DIM
    return (jnp.asarray(np.concatenate(cos * reps, axis=1), F32),
            jnp.asarray(np.concatenate(sin * reps, axis=1), F32))


def _pair_heads(a, axis):
    heads = jnp.split(a, ATT_HEADS, axis=axis)
    return jnp.concatenate([heads[h] for m in range(ATT_GROUP) for h in (m, ATT_GROUP + m)], axis=axis)


def kernel(x, c, ctx, c_ctx, w_ada, b_ada, norm_pre, norm_post, ffn1_w_in, ffn1_w_out, ffn2_w_in, ffn2_w_out,
           mix_w_in, attn_sink, hgrn_lb_fwd, hgrn_lb_bwd, hgrn_norm, w_o_attn, w_o_hgrn, w_out):
    batch, seq, d = x.shape
    ctx_len = ctx.shape[1]
    depth = w_ada.shape[0]
    assert depth == 1, "single-layer problem: the context stream is only read, never written back"
    hw = hgrn_norm.shape[1]
    tm = FFN_SPLIT * ROW_HALF
    tiles_per_batch = seq // tm
    bf = lambda a: a.astype(MXU_DTYPE)

    mod_rows = -(-(batch + 1) // 8) * 8
    cs = jnp.concatenate([c, c_ctx[None, :], jnp.zeros((mod_rows - batch - 1, d), F32)], axis=0)
    mod3 = _modulation(cs, w_ada[0], b_ada[0]).reshape(mod_rows, N_MOD, d)
    ctx_row = batch

    xl = x.reshape(batch * seq, d)
    xc = ctx.reshape(batch * ctx_len, d)
    pre = [norm_pre[0, j][None, :] for j in range(3)]
    post = [norm_post[0, j][None, :] for j in range(3)]

    w1_in, w1_out = bf(ffn1_w_in[0]), bf(ffn1_w_out[0])
    xl = _ffn(xl, mod3, lambda i: i // tiles_per_batch, 0, pre[0], post[0], w1_in, w1_out, tm)
    xc = _ffn(xc, mod3, lambda i: ctx_row, 0, pre[0], post[0], w1_in, w1_out, tm)

    wm = mix_w_in[0]
    kv0 = ATT_WIDTH
    hg0 = kv0 + 2 * ATT_KV_WIDTH
    gate0 = hg0 + 4 * hw
    w_lat = bf(jnp.concatenate([_pair_heads(wm[:, :kv0], 1), wm[:, kv0:gate0]], axis=1))
    w_ctx = bf(jnp.concatenate([wm[:, kv0:hg0], wm[:, hg0 + hw:gate0]], axis=1))
    cos, sin = _rope_tables(seq)
    q, kv, hq, lff, kff, lfb, kfb, hv = _inproj(
        xl, mod3, tiles_per_batch, pre[1], cos, sin, hgrn_lb_fwd, hgrn_lb_bwd, w_lat, tm)
    kvc, lffc, kffc, lfbc, kfbc, hvc = _inproj_ctx(xc, mod3, ctx_row, pre[1], hgrn_lb_fwd, hgrn_lb_bwd, w_ctx,
                                                   ROW_HALF)

    o_att = _attention(attn_sink[0], q, kv, kvc, batch)
    o_hg = _hgrn(hq, hv, lff, kff, lfb, kfb, hvc, lffc, kffc, lfbc, kfbc, batch)

    woa = bf(_pair_heads(w_o_attn[0], 0))
    xl = _tail(xl, mod3, tiles_per_batch, norm_pre[0], norm_post[0], hgrn_norm, o_att, o_hg,
               bf(wm[:, gate0:]), woa, bf(w_o_hgrn[0]), bf(w_out[0]), bf(ffn2_w_in[0]), bf(ffn2_w_out[0]), tm)
    return xl.reshape(batch, seq, d)
```

```python
import functools

import jax
import jax.numpy as jnp
import numpy as np
from jax import lax
from jax.experimental import pallas as pl
from jax.experimental.pallas import tpu as pltpu

F32 = jnp.float32
MXU_DTYPE = jnp.bfloat16

N_MOD = 9
GRID_W = 64
ATT_HEADS = 8
ATT_KV_HEADS = 2
ATT_GROUP = ATT_HEADS // ATT_KV_HEADS
ATT_HEAD_DIM = 64
ATT_WIDTH = ATT_HEADS * ATT_HEAD_DIM
ATT_KV_WIDTH = ATT_KV_HEADS * ATT_HEAD_DIM
WINDOW = 128
ATT_BLOCK = 128
ATT_SLABS = 1
ATT_QBLOCKS = 8
ATT_LOOKAHEAD = 3
ROPE_BASE = 10000.0
HG_EXPAND = 128
FFN_SPLIT = 2
ROW_HALF = 256
EPS = 1e-6
NEG_INF = -1e30
LOG2E = 1.4426950408889634

LANES = 128
HG_CHUNK = 64
HG_SUB = 16
HG_HEADS = 4
HG_UNROLL = 4
HG_UNROLL_EXACT = 2
N_OFF_SEGMENTS = 3
HG_SAFE_LOG2 = 120.0
VMEM_LIMIT = 56 * 1024 * 1024


def _sigmoid(x):
    return 1.0 / (1.0 + jnp.exp(-x))


def _silu(x):
    return x * _sigmoid(x)


def _rms(x, g):
    ms = jnp.mean(x * x, axis=-1, keepdims=True)
    return x * lax.rsqrt(ms + EPS) * g


def _mm(a, b):
    return jnp.dot(a.astype(MXU_DTYPE), b.astype(MXU_DTYPE), preferred_element_type=F32)


def _mm_nt(a, b):
    return lax.dot_general(a.astype(MXU_DTYPE), b.astype(MXU_DTYPE), (((1,), (1,)), ((), ())),
                           preferred_element_type=F32)


def _mm_tn(a, b):
    return lax.dot_general(a.astype(MXU_DTYPE), b.astype(MXU_DTYPE), (((0,), (0,)), ((), ())),
                           preferred_element_type=F32)


def _resident(shape):
    nd = len(shape)
    return pl.BlockSpec(shape, lambda *_: (0,) * nd, pipeline_mode=pl.Buffered(1))


def _params(n_axes):
    return pltpu.CompilerParams(dimension_semantics=("arbitrary",) * n_axes,
                                vmem_limit_bytes=VMEM_LIMIT)


def _mod_kernel(c_ref, w_ref, b_ref, o_ref):
    o_ref[...] = _mm(_silu(c_ref[...]), w_ref[...]) + b_ref[...]


def _modulation(cs, w_ada, b_ada):
    rows, d = cs.shape
    n = w_ada.shape[1]
    tn = n // 8
    return pl.pallas_call(
        _mod_kernel,
        grid=(n // tn,),
        in_specs=[pl.BlockSpec((rows, d), lambda j: (0, 0)),
                  pl.BlockSpec((d, tn), lambda j: (0, j)),
                  pl.BlockSpec((1, tn), lambda j: (0, j))],
        out_specs=pl.BlockSpec((rows, tn), lambda j: (0, j)),
        out_shape=jax.ShapeDtypeStruct((rows, n), F32),
        compiler_params=_params(1),
        name="modulation",
    )(cs, w_ada, b_ada.reshape(1, n))


def _ffn_kernel(x_ref, mod_ref, gpre_ref, gpost_ref, win_ref, wout_ref, o_ref, *, j0, d_ff):
    shift, scale, gate = mod_ref[j0:j0 + 1, :], mod_ref[j0 + 1:j0 + 2, :], mod_ref[j0 + 2:j0 + 3, :]
    th = x_ref.shape[0] // FFN_SPLIT
    halves = [slice(i * th, (i + 1) * th) for i in range(FFN_SPLIT)]
    xs = [x_ref[r, :] for r in halves]
    hs = [(_rms(x, gpre_ref[...]) * (1.0 + scale) + shift).astype(MXU_DTYPE) for x in xs]
    gus = [_mm(h, win_ref[...]) for h in hs]
    acts = [(_silu(gu[:, :d_ff]) * gu[:, d_ff:]).astype(MXU_DTYPE) for gu in gus]
    ys = [_mm(act, wout_ref[...]) for act in acts]
    for r, x, y in zip(halves, xs, ys):
        o_ref[r, :] = x + 0.5 * gate * _rms(y, gpost_ref[...])


def _ffn(x, mod3, mod_row, j0, g_pre, g_post, w_in, w_out, tm):
    rows, d = x.shape
    assert rows % tm == 0, (rows, tm)
    d_ff = w_out.shape[0]
    return pl.pallas_call(
        functools.partial(_ffn_kernel, j0=j0, d_ff=d_ff),
        grid=(rows // tm,),
        in_specs=[pl.BlockSpec((tm, d), lambda i: (i, 0)),
                  pl.BlockSpec((None, N_MOD, d), lambda i: (mod_row(i), 0, 0)),
                  _resident((1, d)), _resident((1, d)),
                  _resident(w_in.shape), _resident(w_out.shape)],
        out_specs=pl.BlockSpec((tm, d), lambda i: (i, 0)),
        out_shape=jax.ShapeDtypeStruct((rows, d), F32),
        compiler_params=_params(1),
        name="ffn",
    )(x, mod3, g_pre, g_post, w_in, w_out)


def _lower_bound(lb_ref):
    a = lb_ref[...]
    e = jnp.exp(a - jnp.max(a, axis=0, keepdims=True))
    return e[0:1, :] / jnp.sum(e, axis=0, keepdims=True)


def _forget(z, lb):
    f = lb + (1.0 - lb) * _sigmoid(z)
    return jnp.log2(f), 1.0 - f


def _rope(x, cos, sin):
    w = x.shape[1]
    reps = w // LANES
    cos_w = jnp.concatenate([cos] * reps, axis=1) if reps > 1 else cos
    sin_w = jnp.concatenate([sin] * reps, axis=1) if reps > 1 else sin
    lane = lax.broadcasted_iota(jnp.int32, x.shape, 1)
    quarter = ATT_HEAD_DIM // 4
    first = (lane % (2 * quarter)) < quarter
    partner = jnp.where(first, pltpu.roll(x, w - quarter, 1), pltpu.roll(x, quarter, 1))
    return x * cos_w + partner * sin_w


def _inproj_kernel(x_ref, mod_ref, gpre_ref, cos_ref, sin_ref, lbf_ref, lbb_ref, w_ref,
                   q_ref, kv_ref, hq_ref, lff_ref, kff_ref, lfb_ref, kfb_ref, hv_ref, *, hw):
    shift, scale = mod_ref[3:4, :], mod_ref[4:5, :]
    th = x_ref.shape[0] // FFN_SPLIT
    halves = [slice(i * th, (i + 1) * th) for i in range(FFN_SPLIT)]
    hs = [(_rms(x_ref[r, :], gpre_ref[...]) * (1.0 + scale) + shift).astype(MXU_DTYPE) for r in halves]
    c0 = 0

    def proj(width):
        nonlocal c0
        outs = [jnp.dot(h, w_ref[:, c0:c0 + width], preferred_element_type=F32) for h in hs]
        c0 += width
        return zip(halves, outs)

    for r, q in proj(ATT_WIDTH):
        q_ref[r, :] = (_rope(q, cos_ref[r, :], sin_ref[r, :]) * (LOG2E * ATT_HEAD_DIM ** -0.5)).astype(q_ref.dtype)
    for r, kv in proj(2 * ATT_KV_WIDTH):
        kv_ref[r, :ATT_KV_WIDTH] = _rope(kv[:, :ATT_KV_WIDTH], cos_ref[r, :], sin_ref[r, :]).astype(kv_ref.dtype)
        kv_ref[r, ATT_KV_WIDTH:] = kv[:, ATT_KV_WIDTH:].astype(kv_ref.dtype)
    for r, z in proj(hw):
        hq_ref[r, :] = _silu(z).astype(hq_ref.dtype)
    for lb_ref, lf_ref, kf_ref in ((lbf_ref, lff_ref, kff_ref), (lbb_ref, lfb_ref, kfb_ref)):
        for r, z in proj(hw):
            lf, kf = _forget(z, _lower_bound(lb_ref))
            lf_ref[r, :] = lf
            kf_ref[r, :] = kf.astype(kf_ref.dtype)
    for r, z in proj(hw):
        hv_ref[r, :] = z.astype(hv_ref.dtype)


def _inproj(x, mod3, tiles_per_batch, g_pre, cos, sin, lb_f, lb_b, w, tm):
    rows, d = x.shape
    assert rows % tm == 0, (rows, tm)
    hw = lb_f.shape[1]
    row = lambda width: pl.BlockSpec((tm, width), lambda i: (i, 0))
    widths = [ATT_WIDTH, 2 * ATT_KV_WIDTH, hw, hw, hw, hw, hw, hw]
    dtypes = [MXU_DTYPE, MXU_DTYPE, MXU_DTYPE, F32, MXU_DTYPE, F32, MXU_DTYPE, MXU_DTYPE]
    return pl.pallas_call(
        functools.partial(_inproj_kernel, hw=hw),
        grid=(rows // tm,),
        in_specs=[row(d),
                  pl.BlockSpec((None, N_MOD, d), lambda i: (i // tiles_per_batch, 0, 0)),
                  _resident((1, d)),
                  pl.BlockSpec((tm, LANES), lambda i: (i % tiles_per_batch, 0)),
                  pl.BlockSpec((tm, LANES), lambda i: (i % tiles_per_batch, 0)),
                  _resident(lb_f.shape), _resident(lb_b.shape), _resident(w.shape)],
        out_specs=[row(wd) for wd in widths],
        out_shape=[jax.ShapeDtypeStruct((rows, wd), dt) for wd, dt in zip(widths, dtypes)],
        compiler_params=_params(1),
        name="inproj",
    )(x, mod3, g_pre, cos, sin, lb_f, lb_b, w)


def _inproj_ctx_kernel(x_ref, mod_ref, gpre_ref, lbf_ref, lbb_ref, w_ref,
                       kv_ref, lff_ref, kff_ref, lfb_ref, kfb_ref, hv_ref, *, hw):
    x = x_ref[...]
    shift, scale = mod_ref[3:4, :], mod_ref[4:5, :]
    h = (_rms(x, gpre_ref[...]) * (1.0 + scale) + shift).astype(MXU_DTYPE)
    kvw = 2 * ATT_KV_WIDTH
    kv_ref[...] = jnp.dot(h, w_ref[:, :kvw], preferred_element_type=F32).astype(kv_ref.dtype)
    lf, kf = _forget(jnp.dot(h, w_ref[:, kvw:kvw + hw], preferred_element_type=F32), _lower_bound(lbf_ref))
    lff_ref[...] = lf
    kff_ref[...] = kf.astype(kff_ref.dtype)
    lf, kf = _forget(jnp.dot(h, w_ref[:, kvw + hw:kvw + 2 * hw], preferred_element_type=F32),
                     _lower_bound(lbb_ref))
    lfb_ref[...] = lf
    kfb_ref[...] = kf.astype(kfb_ref.dtype)
    hv_ref[...] = jnp.dot(h, w_ref[:, kvw + 2 * hw:], preferred_element_type=F32).astype(hv_ref.dtype)


def _inproj_ctx(xc, mod3, ctx_row, g_pre, lb_f, lb_b, w, tm):
    rows, d = xc.shape
    assert rows % tm == 0, (rows, tm)
    hw = lb_f.shape[1]
    row = lambda width: pl.BlockSpec((tm, width), lambda i: (i, 0))
    widths = [2 * ATT_KV_WIDTH, hw, hw, hw, hw, hw]
    dtypes = [MXU_DTYPE, F32, MXU_DTYPE, F32, MXU_DTYPE, MXU_DTYPE]
    return pl.pallas_call(
        functools.partial(_inproj_ctx_kernel, hw=hw),
        grid=(rows // tm,),
        in_specs=[row(d),
                  pl.BlockSpec((None, N_MOD, d), lambda i: (ctx_row, 0, 0)),
                  _resident((1, d)), _resident(lb_f.shape), _resident(lb_b.shape), _resident(w.shape)],
        out_specs=[row(wd) for wd in widths],
        out_shape=[jax.ShapeDtypeStruct((rows, wd), dt) for wd, dt in zip(widths, dtypes)],
        compiler_params=_params(1),
        name="inproj_ctx",
    )(xc, mod3, g_pre, lb_f, lb_b, w)


def _attn_kernel(sink_ref, q_ref, kv_ref, kvc_ref, o_ref, *, seq, band):
    step = pl.program_id(1)
    kvc = kvc_ref[...]
    low_lane = lax.broadcasted_iota(jnp.int32, (ATT_BLOCK, LANES), 1) < ATT_HEAD_DIM
    low_row = lax.broadcasted_iota(jnp.int32, (LANES, ATT_BLOCK), 0) < ATT_HEAD_DIM
    heads_per_group = 2 * ATT_SLABS

    def block_keys(qb):
        n = step * ATT_QBLOCKS + qb
        start = pl.multiple_of(jnp.clip(n * ATT_BLOCK - ATT_BLOCK, 0, seq - band), ATT_BLOCK)
        kvb = kv_ref[pl.ds(start, band), :]
        k_all = jnp.concatenate([kvb[:, :ATT_KV_WIDTH], kvc[:, :ATT_KV_WIDTH]], axis=0)
        v_t = jnp.concatenate([kvb[:, ATT_KV_WIDTH:], kvc[:, ATT_KV_WIDTH:]], axis=0).T
        kpos = start + lax.broadcasted_iota(jnp.int32, (band, ATT_BLOCK), 0)
        qpos = n * ATT_BLOCK + lax.broadcasted_iota(jnp.int32, (band, ATT_BLOCK), 1)
        valid = jnp.abs(kpos - qpos) <= WINDOW
        return k_all, v_t, jnp.concatenate([valid] * heads_per_group, axis=1)

    keys = [block_keys(qb) for qb in range(ATT_QBLOCKS)]

    def unit_scores(qb, m0):
        q_rows, sink_row = [], []
        for m in range(m0, m0 + ATT_SLABS):
            qm = q_ref[qb * ATT_BLOCK:(qb + 1) * ATT_BLOCK, m * LANES:(m + 1) * LANES]
            zero = jnp.zeros_like(qm)
            q_rows += [jnp.where(low_lane, qm, zero), jnp.where(low_lane, zero, qm)]
            sink_row += [jnp.full((1, ATT_BLOCK), sink_ref[m] * LOG2E, F32),
                         jnp.full((1, ATT_BLOCK), sink_ref[ATT_GROUP + m] * LOG2E, F32)]
        return _mm_nt(keys[qb][0], jnp.concatenate(q_rows, axis=0)), jnp.concatenate(sink_row, axis=1)

    units = [(qb, m0) for qb in range(ATT_QBLOCKS) for m0 in range(0, ATT_GROUP, ATT_SLABS)]

    def finish(o_t, den, qb, m0):
        o_t = o_t * (1.0 / den)
        for j, m in enumerate(range(m0, m0 + ATT_SLABS)):
            o_lo = o_t[:, (2 * j) * ATT_BLOCK:(2 * j + 1) * ATT_BLOCK]
            o_hi = o_t[:, (2 * j + 1) * ATT_BLOCK:(2 * j + 2) * ATT_BLOCK]
            o_ref[qb * ATT_BLOCK:(qb + 1) * ATT_BLOCK, m * LANES:(m + 1) * LANES] = (
                jnp.where(low_row, o_lo, o_hi).T.astype(o_ref.dtype))

    pending = [unit_scores(*unit) for unit in units[:ATT_LOOKAHEAD]]
    unfinished = None
    for u, (qb, m0) in enumerate(units):
        if u + ATT_LOOKAHEAD < len(units):
            pending.append(unit_scores(*units[u + ATT_LOOKAHEAD]))
        s_t, sink2 = pending[u]
        _, v_t, valid = keys[qb]
        s_t = jnp.concatenate([jnp.where(valid, s_t[:band], NEG_INF), s_t[band:]], axis=0)
        m = jnp.maximum(jnp.max(s_t, axis=0, keepdims=True), sink2)
        e = jnp.exp2(s_t - m)
        den = jnp.sum(e, axis=0, keepdims=True) + jnp.exp2(sink2 - m)
        o_t = _mm(v_t, e)
        if unfinished is not None:
            finish(*unfinished)
        unfinished = (o_t, den, qb, m0)
    finish(*unfinished)


def _attention(sink, q, kv, kvc, batch):
    rows, _ = q.shape
    seq = rows // batch
    ctx_len = kvc.shape[0] // batch
    tq = ATT_QBLOCKS * ATT_BLOCK
    steps = seq // tq
    band = 3 * ATT_BLOCK
    return pl.pallas_call(
        functools.partial(_attn_kernel, seq=seq, band=band),
        grid=(batch, steps),
        in_specs=[pl.BlockSpec(memory_space=pltpu.SMEM),
                  pl.BlockSpec((tq, ATT_WIDTH), lambda b, n: (b * steps + n, 0)),
                  pl.BlockSpec((seq, 2 * ATT_KV_WIDTH), lambda b, n: (b, 0)),
                  pl.BlockSpec((ctx_len, 2 * ATT_KV_WIDTH), lambda b, n: (b, 0))],
        out_specs=pl.BlockSpec((tq, ATT_WIDTH), lambda b, n: (b * steps + n, 0)),
        out_shape=jax.ShapeDtypeStruct((rows, ATT_WIDTH), MXU_DTYPE),
        compiler_params=_params(2),
        name="attention",
    )(sink, q, kv, kvc)


def _cum_decays(lf_fwd, lf_bwd, tri):
    c = lf_fwd.shape[0]
    lf = jnp.concatenate([lf_fwd, lf_bwd], axis=1)
    hi = lf.astype(jnp.bfloat16)
    r1 = lf - hi.astype(F32)
    mid = r1.astype(jnp.bfloat16)
    lo = (r1 - mid.astype(F32)).astype(jnp.bfloat16)
    both = jnp.dot(tri, jnp.concatenate([hi, mid, lo], axis=0), preferred_element_type=F32)
    return [both[:c, :LANES], both[c:, LANES:]]


def _hg_state_update(st, k, v, b, b_end):
    k_dec = k * jnp.exp2(b_end - b)
    return st * jnp.exp2(b_end) + _mm_tn(v, k_dec)


def _hg_diag_tiles(q4, k4, b4, rev):
    nblk, sub, _ = q4.shape
    half = sub // 2
    tiles = []
    for s in range(sub):
        ks, bs = k4[:, s:s + 1, :], b4[:, s:s + 1, :]
        halves = []
        for hf in range(2):
            needed = (hf == 0 or s >= half) if rev else (hf == 1 or s < half)
            if needed:
                rows = slice(hf * half, (hf + 1) * half)
                halves.append(q4[:, rows] * ks * jnp.exp2(jnp.minimum(b4[:, rows] - bs, 0.0)))
            else:
                halves.append(jnp.zeros((nblk, half, LANES), F32))
        tiles.append(jnp.concatenate(halves, axis=1).reshape(nblk * sub, LANES).astype(MXU_DTYPE))
    return jnp.concatenate(tiles, axis=1)


def _hg_local(q, k, b, rev, exact):
    c, sub = HG_CHUNK, HG_SUB
    nblk = c // sub
    q4, k4, b4 = (a.reshape(nblk, sub, LANES) for a in (q, k, b))
    zero = jnp.zeros((sub, LANES), F32)

    def factors(q_blocks, k_blocks, anchor):
        ba = b[anchor:anchor + 1]
        qs = [q4[i] * jnp.exp2(b4[i] - ba) if i in q_blocks else zero for i in range(nblk)]
        ks = [k4[i] * jnp.exp2(ba - b4[i]) if i in k_blocks else zero for i in range(nblk)]
        return jnp.concatenate(qs, axis=0), jnp.concatenate(ks, axis=0)

    if rev:
        pairs = [factors((0,), (1,), sub), factors((2,), (3,), 3 * sub), factors((0, 1), (2, 3), 2 * sub)]
    else:
        pairs = [factors((1,), (0,), sub - 1), factors((3,), (2,), 3 * sub - 1), factors((2, 3), (0, 1), 2 * sub - 1)]
    if not exact:
        anchors = [b4[i, (sub - 1 if rev else 0):(sub if rev else 1), :] for i in range(nblk)]
        pairs.append((jnp.concatenate([q4[i] * jnp.exp2(b4[i] - anchors[i]) for i in range(nblk)], axis=0),
                      jnp.concatenate([k4[i] * jnp.exp2(anchors[i] - b4[i]) for i in range(nblk)], axis=0)))
    qcat = jnp.concatenate([p[0] for p in pairs], axis=1).astype(MXU_DTYPE)
    kcat = jnp.concatenate([p[1] for p in pairs], axis=1).astype(MXU_DTYPE)
    return (_hg_diag_tiles(q4, k4, b4, rev) if exact else None), qcat, kcat


def _hgrn_kernel(q_ref, v_ref, lff_ref, kff_ref, lfb_ref, kfb_ref,
                 vc_ref, lffc_ref, kffc_ref, lfbc_ref, kfbc_ref, sel_ref, o_ref,
                 s_ref, z_ref, qc_ref, kc_ref, b_ref):
    c, sub = HG_CHUNK, HG_SUB
    seq, ctx_len = q_ref.shape[0], vc_ref.shape[0]
    n, nc = seq // c, ctx_len // c
    r = lax.broadcasted_iota(jnp.int32, (c, c), 0)
    s = lax.broadcasted_iota(jnp.int32, (c, c), 1)
    tri = jnp.concatenate([jnp.concatenate([(s <= r).astype(jnp.bfloat16)] * 3, axis=1),
                           jnp.concatenate([(s >= r).astype(jnp.bfloat16)] * 3, axis=1)], axis=0)
    row = lax.broadcasted_iota(jnp.int32, (c, LANES), 0)
    lane = lax.broadcasted_iota(jnp.int32, (c, LANES), 1)
    own = (lane // sub) == (row // sub)
    dmask_f = own & ((lane % sub) <= (row % sub))
    dmask_b = own & ((lane % sub) >= (row % sub))
    pad = jnp.zeros((LANES - c, LANES), MXU_DTYPE)

    heads = [slice(hd * LANES, (hd + 1) * LANES) for hd in range(q_ref.shape[1] // LANES)]
    dirs = ((kff_ref, dmask_f, False), (kfb_ref, dmask_b, True))

    for hd, ln in enumerate(heads):
        sf, sb = jnp.zeros(s_ref.shape[2:], F32), jnp.zeros(s_ref.shape[2:], F32)
        for i in range(nc):
            rf, rb = slice(i * c, (i + 1) * c), slice((nc - 1 - i) * c, (nc - i) * c)
            bf, bb = _cum_decays(lffc_ref[rf, ln], lfbc_ref[rb, ln], tri)
            sf = _hg_state_update(sf, kffc_ref[rf, ln].astype(F32), vc_ref[rf, ln], bf, bf[c - 1:c, :])
            sb = _hg_state_update(sb, kfbc_ref[rb, ln].astype(F32), vc_ref[rb, ln], bb, bb[0:1, :])
        s_ref[hd, 0] = sf
        s_ref[hd, 1] = sb

    def chunk_rows(i, rev):
        return pl.ds(pl.multiple_of(((n - 1 - i) if rev else i) * c, c), c)

    def cum_decays(i):
        return [_cum_decays(lff_ref[chunk_rows(i, False), ln], lfb_ref[chunk_rows(i, True), ln], tri)
                for ln in heads]

    def stage_local(i, bs, slot, exact):
        for hd, ln in enumerate(heads):
            for j, (k_ref, _, rev) in enumerate(dirs):
                rows = chunk_rows(i, rev)
                zcat, qcat, kcat = _hg_local(q_ref[rows, ln].astype(F32), k_ref[rows, ln].astype(F32),
                                             bs[hd][j], rev, exact)
                if exact:
                    z_ref[slot, hd, j * c:(j + 1) * c, :] = zcat
                qc_ref[slot, hd, j, :, 0:qcat.shape[1]] = qcat
                kc_ref[slot, hd, j, 0:c, 0:kcat.shape[1]] = kcat
                b_ref[slot, hd, j] = bs[hd][j]

    def step(i, slot, accumulate, exact):
        off = N_OFF_SEGMENTS * LANES
        nxt = jnp.minimum(i + 1, n - 1)
        b_next = cum_decays(nxt)
        started = []
        for hd, ln in enumerate(heads):
            if exact:
                diag_both = jnp.dot(z_ref[slot, hd], sel_ref[...], preferred_element_type=F32)
            for j, (k_ref, dmask, rev) in enumerate(dirs):
                rows = chunk_rows(i, rev)
                q, k = q_ref[rows, ln].astype(F32), k_ref[rows, ln].astype(F32)
                v, b = v_ref[rows, ln], b_ref[slot, hd, j]
                st = s_ref[hd, j]
                b_end = b[0:1, :] if rev else b[c - 1:c, :]
                below = _mm_nt(qc_ref[slot, hd, j, :, 0:off], kc_ref[slot, hd, j, :, 0:off])
                if exact:
                    diag = diag_both[j * c:(j + 1) * c]
                else:
                    diag = _mm_nt(qc_ref[slot, hd, j, :, off:off + LANES], kc_ref[slot, hd, j, :, off:off + LANES])
                rhs = jnp.concatenate([st.T.astype(MXU_DTYPE), v.astype(MXU_DTYPE), pad], axis=0)
                started.append((hd, ln, j, rows, dmask, (q * jnp.exp2(b)).astype(MXU_DTYPE), rhs, below, diag,
                                _hg_state_update(st, k, v, b, b_end)))
        stage_local(nxt, b_next, 1 - slot, exact)
        for hd, ln, j, rows, dmask, q_dec, rhs, below, diag, st_new in started:
            scores = jnp.where(dmask, diag, 0.0) + below
            o = jnp.dot(jnp.concatenate([q_dec, scores.astype(MXU_DTYPE)], axis=1), rhs,
                        preferred_element_type=F32)
            s_ref[hd, j] = st_new
            if accumulate:
                o_ref[rows, ln] += o
            else:
                o_ref[rows, ln] = o

    def scan(exact):
        stage_local(0, cum_decays(0), 0, exact)

        unroll = HG_UNROLL_EXACT if exact else HG_UNROLL

        def steps(ii, carry, accumulate):
            for u in range(unroll):
                step(ii * unroll + u, u % 2, accumulate, exact)
            return carry

        half_trips = n // (2 * unroll)
        lax.fori_loop(0, half_trips, functools.partial(steps, accumulate=False), 0)
        lax.fori_loop(half_trips, 2 * half_trips, functools.partial(steps, accumulate=True), 0)

    kc_ref[...] = jnp.zeros_like(kc_ref)
    steepest = jnp.minimum(jnp.min(lff_ref[...]), jnp.min(lfb_ref[...]))
    lax.cond(steepest * (sub - 1) >= -HG_SAFE_LOG2, lambda: scan(False), lambda: scan(True))


def _hgrn(hq, hv, lff, kff, lfb, kfb, hvc, lffc, kffc, lfbc, kfbc, batch):
    rows, hw = hq.shape
    seq = rows // batch
    ctx_len = hvc.shape[0] // batch
    heads = hw // HG_EXPAND
    assert seq % (2 * HG_UNROLL * HG_CHUNK) == 0 and ctx_len % HG_CHUNK == 0
    assert HG_UNROLL % HG_UNROLL_EXACT == 0 and HG_UNROLL_EXACT % 2 == 0
    n_seg = N_OFF_SEGMENTS + 1
    src = np.arange(HG_SUB * LANES, dtype=np.int32)[:, None] // LANES
    sel = jnp.asarray(src == np.arange(LANES, dtype=np.int32)[None, :] % HG_SUB, MXU_DTYPE)
    assert heads % HG_HEADS == 0
    width = HG_HEADS * HG_EXPAND
    lat = pl.BlockSpec((seq, width), lambda b, h: (b, h))
    ctx = pl.BlockSpec((ctx_len, width), lambda b, h: (b, h))
    return pl.pallas_call(
        _hgrn_kernel,
        grid=(batch, heads // HG_HEADS),
        in_specs=[lat] * 6 + [ctx] * 5 + [_resident(sel.shape)],
        out_specs=lat,
        out_shape=jax.ShapeDtypeStruct((rows, hw), F32),
        scratch_shapes=[
            pltpu.VMEM((HG_HEADS, 2, HG_EXPAND, HG_EXPAND), F32),
            pltpu.VMEM((2, HG_HEADS, 2 * HG_CHUNK, HG_SUB * LANES), MXU_DTYPE),
            pltpu.VMEM((2, HG_HEADS, 2, HG_CHUNK, n_seg * LANES), MXU_DTYPE),
            pltpu.VMEM((2, HG_HEADS, 2, LANES, n_seg * LANES), MXU_DTYPE),
            pltpu.VMEM((2, HG_HEADS, 2, HG_CHUNK, LANES), F32)],
        compiler_params=_params(2),
        name="hgrn",
    )(hq, hv, lff, kff, lfb, kfb, hvc, lffc, kffc, lfbc, kfbc, sel)


def _tail_kernel(x_ref, mod_ref, pre_ref, post_ref, ghn_ref, oa_ref, oh_ref,
                 wg_ref, woa_ref, woh_ref, wout_ref, win_ref, wo2_ref, o_ref, *, hw, d_ff):
    d = x_ref.shape[1]
    mod = lambda j: mod_ref[j:j + 1, :]
    bf = lambda a: a.astype(MXU_DTYPE)
    th = x_ref.shape[0] // FFN_SPLIT
    halves = [slice(i * th, (i + 1) * th) for i in range(FFN_SPLIT)]
    xs = [x_ref[r, :] for r in halves]
    gs = [_mm(bf(_rms(x, pre_ref[1:2, :]) * (1.0 + mod(4)) + mod(3)), wg_ref[...]) for x in xs]
    rs = [bf(_rms(oh_ref[r, :], ghn_ref[...]) * _silu(g[:, :hw])) for r, g in zip(halves, gs)]
    zs = [bf(_sigmoid(g[:, hw:hw + d]) * _mm(oa_ref[r, :], woa_ref[...])
             + _sigmoid(g[:, hw + d:]) * _mm(rr, woh_ref[...])) for r, g, rr in zip(halves, gs, rs)]
    xs = [x + mod(5) * _rms(_mm(z, wout_ref[...]), post_ref[1:2, :]) for x, z in zip(xs, zs)]
    hs = [bf(_rms(x, pre_ref[2:3, :]) * (1.0 + mod(7)) + mod(6)) for x in xs]
    gus = [_mm(h, win_ref[...]) for h in hs]
    acts = [bf(_silu(gu[:, :d_ff]) * gu[:, d_ff:]) for gu in gus]
    ys = [_mm(act, wo2_ref[...]) for act in acts]
    for r, x, y in zip(halves, xs, ys):
        o_ref[r, :] = x + 0.5 * mod(8) * _rms(y, post_ref[2:3, :])


def _tail(x, mod3, tiles_per_batch, pre, post, g_hnorm, oa, oh, wg, woa, woh, wout, w_in, w_out, tm):
    rows, d = x.shape
    assert rows % tm == 0, (rows, tm)
    row = lambda width: pl.BlockSpec((tm, width), lambda i: (i, 0))
    return pl.pallas_call(
        functools.partial(_tail_kernel, hw=g_hnorm.shape[1], d_ff=w_out.shape[0]),
        grid=(rows // tm,),
        in_specs=[row(d),
                  pl.BlockSpec((None, N_MOD, d), lambda i: (i // tiles_per_batch, 0, 0)),
                  _resident(pre.shape), _resident(post.shape), _resident(g_hnorm.shape),
                  row(oa.shape[1]), row(oh.shape[1])]
                 + [_resident(w.shape) for w in (wg, woa, woh, wout, w_in, w_out)],
        out_specs=row(d),
        out_shape=jax.ShapeDtypeStruct((rows, d), F32),
        compiler_params=_params(1),
        name="merge_ffn",
    )(x, mod3, pre, post, g_hnorm, oa, oh, wg, woa, woh, wout, w_in, w_out)


def _rope_tables(seq):
    quarter = ATT_HEAD_DIM // 4
    t = np.arange(seq, dtype=np.int32)
    freqs = np.float32(ROPE_BASE) ** (-np.arange(quarter, dtype=np.float32) / np.float32(quarter))
    cos, sin = [], []
    for pos in (t // GRID_W, t % GRID_W):
        ang = pos.astype(np.float32)[:, None] * freqs[None, :]
        cos += [np.cos(ang), np.cos(ang)]
        sin += [-np.sin(ang), np.sin(ang)]
    reps = LANES // ATT_HEAD_DIM
    return (jnp.asarray(np.concatenate(cos * reps, axis=1), F32),
            jnp.asarray(np.concatenate(sin * reps, axis=1), F32))


def _pair_heads(a, axis):
    heads = jnp.split(a, ATT_HEADS, axis=axis)
    return jnp.concatenate([heads[h] for m in range(ATT_GROUP) for h in (m, ATT_GROUP + m)], axis=axis)


def kernel(x, c, ctx, c_ctx, w_ada, b_ada, norm_pre, norm_post, ffn1_w_in, ffn1_w_out, ffn2_w_in, ffn2_w_out,
           mix_w_in, attn_sink, hgrn_lb_fwd, hgrn_lb_bwd, hgrn_norm, w_o_attn, w_o_hgrn, w_out):
    batch, seq, d = x.shape
    ctx_len = ctx.shape[1]
    depth = w_ada.shape[0]
    assert depth == 1, "single-layer problem: the context stream is only read, never written back"
    hw = hgrn_norm.shape[1]
    tm = FFN_SPLIT * ROW_HALF
    tiles_per_batch = seq // tm
    bf = lambda a: a.astype(MXU_DTYPE)

    mod_rows = -(-(batch + 1) // 8) * 8
    cs = jnp.concatenate([c, c_ctx[None, :], jnp.zeros((mod_rows - batch - 1, d), F32)], axis=0)
    mod3 = _modulation(cs, w_ada[0], b_ada[0]).reshape(mod_rows, N_MOD, d)
    ctx_row = batch

    xl = x.reshape(batch * seq, d)
    xc = ctx.reshape(batch * ctx_len, d)
    pre = [norm_pre[0, j][None, :] for j in range(3)]
    post = [norm_post[0, j][None, :] for j in range(3)]

    w1_in, w1_out = bf(ffn1_w_in[0]), bf(ffn1_w_out[0])
    xl = _ffn(xl, mod3, lambda i: i // tiles_per_batch, 0, pre[0], post[0], w1_in, w1_out, tm)
    xc = _ffn(xc, mod3, lambda i: ctx_row, 0, pre[0], post[0], w1_in, w1_out, tm)

    wm = mix_w_in[0]
    kv0 = ATT_WIDTH
    hg0 = kv0 + 2 * ATT_KV_WIDTH
    gate0 = hg0 + 4 * hw
    w_lat = bf(jnp.concatenate([_pair_heads(wm[:, :kv0], 1), wm[:, kv0:gate0]], axis=1))
    w_ctx = bf(jnp.concatenate([wm[:, kv0:hg0], wm[:, hg0 + hw:gate0]], axis=1))
    cos, sin = _rope_tables(seq)
    q, kv, hq, lff, kff, lfb, kfb, hv = _inproj(
        xl, mod3, tiles_per_batch, pre[1], cos, sin, hgrn_lb_fwd, hgrn_lb_bwd, w_lat, tm)
    kvc, lffc, kffc, lfbc, kfbc, hvc = _inproj_ctx(xc, mod3, ctx_row, pre[1], hgrn_lb_fwd, hgrn_lb_bwd, w_ctx,
                                                   ROW_HALF)

    o_att = _attention(attn_sink[0], q, kv, kvc, batch)
    o_hg = _hgrn(hq, hv, lff, kff, lfb, kfb, hvc, lffc, kffc, lfbc, kfbc, batch)

    woa = bf(_pair_heads(w_o_attn[0], 0))
    xl = _tail(xl, mod3, tiles_per_batch, norm_pre[0], norm_post[0], hgrn_norm, o_att, o_hg,
               bf(wm[:, gate0:]), woa, bf(w_o_hgrn[0]), bf(w_out[0]), bf(ffn2_w_in[0]), bf(ffn2_w_out[0]), tm)
    return xl.reshape(batch, seq, d)
```

```python
import functools

import jax
import jax.numpy as jnp
import numpy as np
from jax import lax
from jax.experimental import pallas as pl
from jax.experimental.pallas import tpu as pltpu

F32 = jnp.float32
MXU_DTYPE = jnp.bfloat16

N_MOD = 9
GRID_W = 64
ATT_HEADS = 8
ATT_KV_HEADS = 2
ATT_GROUP = ATT_HEADS // ATT_KV_HEADS
ATT_HEAD_DIM = 64
ATT_WIDTH = ATT_HEADS * ATT_HEAD_DIM
ATT_KV_WIDTH = ATT_KV_HEADS * ATT_HEAD_DIM
WINDOW = 128
ATT_BLOCK = 128
ATT_SLABS = 1
ATT_QBLOCKS = 8
ATT_LOOKAHEAD = 3
ROPE_BASE = 10000.0
HG_EXPAND = 128
ROW_HALF = 256
FFN_PARTS = 4
INPROJ_PARTS = 4
TAIL_PARTS = 2
EPS = 1e-6
NEG_INF = -1e30
LOG2E = 1.4426950408889634

LANES = 128
HG_CHUNK = 64
HG_SUB = 16
HG_HEADS = 4
HG_UNROLL = 4
HG_UNROLL_EXACT = 2
N_OFF_SEGMENTS = 3
HG_SAFE_LOG2 = 120.0
VMEM_LIMIT = 56 * 1024 * 1024


def _sigmoid(x):
    return 1.0 / (1.0 + jnp.exp(-x))


def _silu(x):
    return x * _sigmoid(x)


def _rms(x, g):
    ms = jnp.mean(x * x, axis=-1, keepdims=True)
    return x * lax.rsqrt(ms + EPS) * g


def _mm(a, b):
    return jnp.dot(a.astype(MXU_DTYPE), b.astype(MXU_DTYPE), preferred_element_type=F32)


def _mm_nt(a, b):
    return lax.dot_general(a.astype(MXU_DTYPE), b.astype(MXU_DTYPE), (((1,), (1,)), ((), ())),
                           preferred_element_type=F32)


def _mm_tn(a, b):
    return lax.dot_general(a.astype(MXU_DTYPE), b.astype(MXU_DTYPE), (((0,), (0,)), ((), ())),
                           preferred_element_type=F32)


def _resident(shape):
    nd = len(shape)
    return pl.BlockSpec(shape, lambda *_: (0,) * nd, pipeline_mode=pl.Buffered(1))


def _params(n_axes):
    return pltpu.CompilerParams(dimension_semantics=("arbitrary",) * n_axes,
                                vmem_limit_bytes=VMEM_LIMIT)


def _mod_kernel(c_ref, w_ref, b_ref, o_ref):
    o_ref[...] = _mm(_silu(c_ref[...]), w_ref[...]) + b_ref[...]


def _modulation(cs, w_ada, b_ada):
    rows, d = cs.shape
    n = w_ada.shape[1]
    tn = n // 8
    return pl.pallas_call(
        _mod_kernel,
        grid=(n // tn,),
        in_specs=[pl.BlockSpec((rows, d), lambda j: (0, 0)),
                  pl.BlockSpec((d, tn), lambda j: (0, j)),
                  pl.BlockSpec((1, tn), lambda j: (0, j))],
        out_specs=pl.BlockSpec((rows, tn), lambda j: (0, j)),
        out_shape=jax.ShapeDtypeStruct((rows, n), F32),
        compiler_params=_params(1),
        name="modulation",
    )(cs, w_ada, b_ada.reshape(1, n))


def _ffn_kernel(x_ref, mod_ref, gpre_ref, gpost_ref, win_ref, wout_ref, o_ref, *, j0, d_ff):
    shift, scale, gate = mod_ref[j0:j0 + 1, :], mod_ref[j0 + 1:j0 + 2, :], mod_ref[j0 + 2:j0 + 3, :]
    halves = [slice(r0, r0 + ROW_HALF) for r0 in range(0, x_ref.shape[0], ROW_HALF)]
    xs = [x_ref[r, :] for r in halves]
    hs = [(_rms(x, gpre_ref[...]) * (1.0 + scale) + shift).astype(MXU_DTYPE) for x in xs]
    gus = [_mm(h, win_ref[...]) for h in hs]
    acts = [(_silu(gu[:, :d_ff]) * gu[:, d_ff:]).astype(MXU_DTYPE) for gu in gus]
    ys = [_mm(act, wout_ref[...]) for act in acts]
    for r, x, y in zip(halves, xs, ys):
        o_ref[r, :] = x + 0.5 * gate * _rms(y, gpost_ref[...])


def _ffn(x, mod3, mod_row, j0, g_pre, g_post, w_in, w_out, tm):
    rows, d = x.shape
    assert rows % tm == 0, (rows, tm)
    d_ff = w_out.shape[0]
    return pl.pallas_call(
        functools.partial(_ffn_kernel, j0=j0, d_ff=d_ff),
        grid=(rows // tm,),
        in_specs=[pl.BlockSpec((tm, d), lambda i: (i, 0)),
                  pl.BlockSpec((None, N_MOD, d), lambda i: (mod_row(i), 0, 0)),
                  _resident((1, d)), _resident((1, d)),
                  _resident(w_in.shape), _resident(w_out.shape)],
        out_specs=pl.BlockSpec((tm, d), lambda i: (i, 0)),
        out_shape=jax.ShapeDtypeStruct((rows, d), F32),
        compiler_params=_params(1),
        name="ffn",
    )(x, mod3, g_pre, g_post, w_in, w_out)


def _lower_bound(lb_ref):
    a = lb_ref[...]
    e = jnp.exp(a - jnp.max(a, axis=0, keepdims=True))
    return e[0:1, :] / jnp.sum(e, axis=0, keepdims=True)


def _forget(z, lb):
    f = lb + (1.0 - lb) * _sigmoid(z)
    return jnp.log2(f), 1.0 - f


def _rope(x, cos, sin):
    w = x.shape[1]
    reps = w // LANES
    cos_w = jnp.concatenate([cos] * reps, axis=1) if reps > 1 else cos
    sin_w = jnp.concatenate([sin] * reps, axis=1) if reps > 1 else sin
    lane = lax.broadcasted_iota(jnp.int32, x.shape, 1)
    quarter = ATT_HEAD_DIM // 4
    first = (lane % (2 * quarter)) < quarter
    partner = jnp.where(first, pltpu.roll(x, w - quarter, 1), pltpu.roll(x, quarter, 1))
    return x * cos_w + partner * sin_w


def _inproj_kernel(x_ref, mod_ref, gpre_ref, cos_ref, sin_ref, lbf_ref, lbb_ref, w_ref,
                   q_ref, kv_ref, hq_ref, lff_ref, kff_ref, lfb_ref, kfb_ref, hv_ref, *, hw):
    shift, scale = mod_ref[3:4, :], mod_ref[4:5, :]
    halves = [slice(r0, r0 + ROW_HALF) for r0 in range(0, x_ref.shape[0], ROW_HALF)]
    hs = [(_rms(x_ref[r, :], gpre_ref[...]) * (1.0 + scale) + shift).astype(MXU_DTYPE) for r in halves]
    c0 = 0

    def proj(width):
        nonlocal c0
        outs = [jnp.dot(h, w_ref[:, c0:c0 + width], preferred_element_type=F32) for h in hs]
        c0 += width
        return zip(halves, outs)

    for r, q in proj(ATT_WIDTH):
        q_ref[r, :] = (_rope(q, cos_ref[r, :], sin_ref[r, :]) * (LOG2E * ATT_HEAD_DIM ** -0.5)).astype(q_ref.dtype)
    for r, kv in proj(2 * ATT_KV_WIDTH):
        kv_ref[r, :ATT_KV_WIDTH] = _rope(kv[:, :ATT_KV_WIDTH], cos_ref[r, :], sin_ref[r, :]).astype(kv_ref.dtype)
        kv_ref[r, ATT_KV_WIDTH:] = kv[:, ATT_KV_WIDTH:].astype(kv_ref.dtype)
    for r, z in proj(hw):
        hq_ref[r, :] = _silu(z).astype(hq_ref.dtype)
    for lb_ref, lf_ref, kf_ref in ((lbf_ref, lff_ref, kff_ref), (lbb_ref, lfb_ref, kfb_ref)):
        for r, z in proj(hw):
            lf, kf = _forget(z, _lower_bound(lb_ref))
            lf_ref[r, :] = lf
            kf_ref[r, :] = kf.astype(kf_ref.dtype)
    for r, z in proj(hw):
        hv_ref[r, :] = z.astype(hv_ref.dtype)


def _inproj(x, mod3, tiles_per_batch, g_pre, cos, sin, lb_f, lb_b, w, tm):
    rows, d = x.shape
    assert rows % tm == 0, (rows, tm)
    hw = lb_f.shape[1]
    row = lambda width: pl.BlockSpec((tm, width), lambda i: (i, 0))
    widths = [ATT_WIDTH, 2 * ATT_KV_WIDTH, hw, hw, hw, hw, hw, hw]
    dtypes = [MXU_DTYPE, MXU_DTYPE, MXU_DTYPE, F32, MXU_DTYPE, F32, MXU_DTYPE, MXU_DTYPE]
    return pl.pallas_call(
        functools.partial(_inproj_kernel, hw=hw),
        grid=(rows // tm,),
        in_specs=[row(d),
                  pl.BlockSpec((None, N_MOD, d), lambda i: (i // tiles_per_batch, 0, 0)),
                  _resident((1, d)),
                  pl.BlockSpec((tm, LANES), lambda i: (i % tiles_per_batch, 0)),
                  pl.BlockSpec((tm, LANES), lambda i: (i % tiles_per_batch, 0)),
                  _resident(lb_f.shape), _resident(lb_b.shape), _resident(w.shape)],
        out_specs=[row(wd) for wd in widths],
        out_shape=[jax.ShapeDtypeStruct((rows, wd), dt) for wd, dt in zip(widths, dtypes)],
        compiler_params=_params(1),
        name="inproj",
    )(x, mod3, g_pre, cos, sin, lb_f, lb_b, w)


def _inproj_ctx_kernel(x_ref, mod_ref, gpre_ref, lbf_ref, lbb_ref, w_ref,
                       kv_ref, lff_ref, kff_ref, lfb_ref, kfb_ref, hv_ref, *, hw):
    x = x_ref[...]
    shift, scale = mod_ref[3:4, :], mod_ref[4:5, :]
    h = (_rms(x, gpre_ref[...]) * (1.0 + scale) + shift).astype(MXU_DTYPE)
    kvw = 2 * ATT_KV_WIDTH
    kv_ref[...] = jnp.dot(h, w_ref[:, :kvw], preferred_element_type=F32).astype(kv_ref.dtype)
    lf, kf = _forget(jnp.dot(h, w_ref[:, kvw:kvw + hw], preferred_element_type=F32), _lower_bound(lbf_ref))
    lff_ref[...] = lf
    kff_ref[...] = kf.astype(kff_ref.dtype)
    lf, kf = _forget(jnp.dot(h, w_ref[:, kvw + hw:kvw + 2 * hw], preferred_element_type=F32),
                     _lower_bound(lbb_ref))
    lfb_ref[...] = lf
    kfb_ref[...] = kf.astype(kfb_ref.dtype)
    hv_ref[...] = jnp.dot(h, w_ref[:, kvw + 2 * hw:], preferred_element_type=F32).astype(hv_ref.dtype)


def _inproj_ctx(xc, mod3, ctx_row, g_pre, lb_f, lb_b, w, tm):
    rows, d = xc.shape
    assert rows % tm == 0, (rows, tm)
    hw = lb_f.shape[1]
    row = lambda width: pl.BlockSpec((tm, width), lambda i: (i, 0))
    widths = [2 * ATT_KV_WIDTH, hw, hw, hw, hw, hw]
    dtypes = [MXU_DTYPE, F32, MXU_DTYPE, F32, MXU_DTYPE, MXU_DTYPE]
    return pl.pallas_call(
        functools.partial(_inproj_ctx_kernel, hw=hw),
        grid=(rows // tm,),
        in_specs=[row(d),
                  pl.BlockSpec((None, N_MOD, d), lambda i: (ctx_row, 0, 0)),
                  _resident((1, d)), _resident(lb_f.shape), _resident(lb_b.shape), _resident(w.shape)],
        out_specs=[row(wd) for wd in widths],
        out_shape=[jax.ShapeDtypeStruct((rows, wd), dt) for wd, dt in zip(widths, dtypes)],
        compiler_params=_params(1),
        name="inproj_ctx",
    )(xc, mod3, g_pre, lb_f, lb_b, w)


def _attn_kernel(sink_ref, q_ref, kv_ref, kvc_ref, o_ref, *, seq, band):
    step = pl.program_id(1)
    kvc = kvc_ref[...]
    low_lane = lax.broadcasted_iota(jnp.int32, (ATT_BLOCK, LANES), 1) < ATT_HEAD_DIM
    low_row = lax.broadcasted_iota(jnp.int32, (LANES, ATT_BLOCK), 0) < ATT_HEAD_DIM
    heads_per_group = 2 * ATT_SLABS

    def block_keys(qb):
        n = step * ATT_QBLOCKS + qb
        start = pl.multiple_of(jnp.clip(n * ATT_BLOCK - ATT_BLOCK, 0, seq - band), ATT_BLOCK)
        kvb = kv_ref[pl.ds(start, band), :]
        k_all = jnp.concatenate([kvb[:, :ATT_KV_WIDTH], kvc[:, :ATT_KV_WIDTH]], axis=0)
        v_t = jnp.concatenate([kvb[:, ATT_KV_WIDTH:], kvc[:, ATT_KV_WIDTH:]], axis=0).T
        kpos = start + lax.broadcasted_iota(jnp.int32, (band, ATT_BLOCK), 0)
        qpos = n * ATT_BLOCK + lax.broadcasted_iota(jnp.int32, (band, ATT_BLOCK), 1)
        valid = jnp.abs(kpos - qpos) <= WINDOW
        return k_all, v_t, jnp.concatenate([valid] * heads_per_group, axis=1)

    keys = [block_keys(qb) for qb in range(ATT_QBLOCKS)]

    def unit_scores(qb, m0):
        q_rows, sink_row = [], []
        for m in range(m0, m0 + ATT_SLABS):
            qm = q_ref[qb * ATT_BLOCK:(qb + 1) * ATT_BLOCK, m * LANES:(m + 1) * LANES]
            zero = jnp.zeros_like(qm)
            q_rows += [jnp.where(low_lane, qm, zero), jnp.where(low_lane, zero, qm)]
            sink_row += [jnp.full((1, ATT_BLOCK), sink_ref[m] * LOG2E, F32),
                         jnp.full((1, ATT_BLOCK), sink_ref[ATT_GROUP + m] * LOG2E, F32)]
        return _mm_nt(keys[qb][0], jnp.concatenate(q_rows, axis=0)), jnp.concatenate(sink_row, axis=1)

    units = [(qb, m0) for qb in range(ATT_QBLOCKS) for m0 in range(0, ATT_GROUP, ATT_SLABS)]

    def finish(o_t, den, qb, m0):
        o_t = o_t * (1.0 / den)
        for j, m in enumerate(range(m0, m0 + ATT_SLABS)):
            o_lo = o_t[:, (2 * j) * ATT_BLOCK:(2 * j + 1) * ATT_BLOCK]
            o_hi = o_t[:, (2 * j + 1) * ATT_BLOCK:(2 * j + 2) * ATT_BLOCK]
            o_ref[qb * ATT_BLOCK:(qb + 1) * ATT_BLOCK, m * LANES:(m + 1) * LANES] = (
                jnp.where(low_row, o_lo, o_hi).T.astype(o_ref.dtype))

    pending = [unit_scores(*unit) for unit in units[:ATT_LOOKAHEAD]]
    unfinished = None
    for u, (qb, m0) in enumerate(units):
        if u + ATT_LOOKAHEAD < len(units):
            pending.append(unit_scores(*units[u + ATT_LOOKAHEAD]))
        s_t, sink2 = pending[u]
        _, v_t, valid = keys[qb]
        s_t = jnp.concatenate([jnp.where(valid, s_t[:band], NEG_INF), s_t[band:]], axis=0)
        m = jnp.maximum(jnp.max(s_t, axis=0, keepdims=True), sink2)
        e = jnp.exp2(s_t - m)
        den = jnp.sum(e, axis=0, keepdims=True) + jnp.exp2(sink2 - m)
        o_t = _mm(v_t, e)
        if unfinished is not None:
            finish(*unfinished)
        unfinished = (o_t, den, qb, m0)
    finish(*unfinished)


def _attention(sink, q, kv, kvc, batch):
    rows, _ = q.shape
    seq = rows // batch
    ctx_len = kvc.shape[0] // batch
    tq = ATT_QBLOCKS * ATT_BLOCK
    steps = seq // tq
    band = 3 * ATT_BLOCK
    return pl.pallas_call(
        functools.partial(_attn_kernel, seq=seq, band=band),
        grid=(batch, steps),
        in_specs=[pl.BlockSpec(memory_space=pltpu.SMEM),
                  pl.BlockSpec((tq, ATT_WIDTH), lambda b, n: (b * steps + n, 0)),
                  pl.BlockSpec((seq, 2 * ATT_KV_WIDTH), lambda b, n: (b, 0)),
                  pl.BlockSpec((ctx_len, 2 * ATT_KV_WIDTH), lambda b, n: (b, 0))],
        out_specs=pl.BlockSpec((tq, ATT_WIDTH), lambda b, n: (b * steps + n, 0)),
        out_shape=jax.ShapeDtypeStruct((rows, ATT_WIDTH), MXU_DTYPE),
        compiler_params=_params(2),
        name="attention",
    )(sink, q, kv, kvc)


def _cum_decays(lf_fwd, lf_bwd, tri):
    c = lf_fwd.shape[0]
    lf = jnp.concatenate([lf_fwd, lf_bwd], axis=1)
    hi = lf.astype(jnp.bfloat16)
    r1 = lf - hi.astype(F32)
    mid = r1.astype(jnp.bfloat16)
    lo = (r1 - mid.astype(F32)).astype(jnp.bfloat16)
    both = jnp.dot(tri, jnp.concatenate([hi, mid, lo], axis=0), preferred_element_type=F32)
    return [both[:c, :LANES], both[c:, LANES:]]


def _hg_state_update(st, k, v, b, b_end):
    k_dec = k * jnp.exp2(b_end - b)
    return st * jnp.exp2(b_end) + _mm_tn(v, k_dec)


def _hg_diag_tiles(q4, k4, b4, rev):
    nblk, sub, _ = q4.shape
    half = sub // 2
    tiles = []
    for s in range(sub):
        ks, bs = k4[:, s:s + 1, :], b4[:, s:s + 1, :]
        halves = []
        for hf in range(2):
            needed = (hf == 0 or s >= half) if rev else (hf == 1 or s < half)
            if needed:
                rows = slice(hf * half, (hf + 1) * half)
                halves.append(q4[:, rows] * ks * jnp.exp2(jnp.minimum(b4[:, rows] - bs, 0.0)))
            else:
                halves.append(jnp.zeros((nblk, half, LANES), F32))
        tiles.append(jnp.concatenate(halves, axis=1).reshape(nblk * sub, LANES).astype(MXU_DTYPE))
    return jnp.concatenate(tiles, axis=1)


def _hg_local(q, k, b, rev, exact):
    c, sub = HG_CHUNK, HG_SUB
    nblk = c // sub
    q4, k4, b4 = (a.reshape(nblk, sub, LANES) for a in (q, k, b))
    zero = jnp.zeros((sub, LANES), F32)

    def factors(q_blocks, k_blocks, anchor):
        ba = b[anchor:anchor + 1]
        qs = [q4[i] * jnp.exp2(b4[i] - ba) if i in q_blocks else zero for i in range(nblk)]
        ks = [k4[i] * jnp.exp2(ba - b4[i]) if i in k_blocks else zero for i in range(nblk)]
        return jnp.concatenate(qs, axis=0), jnp.concatenate(ks, axis=0)

    if rev:
        pairs = [factors((0,), (1,), sub), factors((2,), (3,), 3 * sub), factors((0, 1), (2, 3), 2 * sub)]
    else:
        pairs = [factors((1,), (0,), sub - 1), factors((3,), (2,), 3 * sub - 1), factors((2, 3), (0, 1), 2 * sub - 1)]
    if not exact:
        anchors = [b4[i, (sub - 1 if rev else 0):(sub if rev else 1), :] for i in range(nblk)]
        pairs.append((jnp.concatenate([q4[i] * jnp.exp2(b4[i] - anchors[i]) for i in range(nblk)], axis=0),
                      jnp.concatenate([k4[i] * jnp.exp2(anchors[i] - b4[i]) for i in range(nblk)], axis=0)))
    qcat = jnp.concatenate([p[0] for p in pairs], axis=1).astype(MXU_DTYPE)
    kcat = jnp.concatenate([p[1] for p in pairs], axis=1).astype(MXU_DTYPE)
    return (_hg_diag_tiles(q4, k4, b4, rev) if exact else None), qcat, kcat


def _hgrn_kernel(q_ref, v_ref, lff_ref, kff_ref, lfb_ref, kfb_ref,
                 vc_ref, lffc_ref, kffc_ref, lfbc_ref, kfbc_ref, sel_ref, o_ref,
                 s_ref, z_ref, qc_ref, kc_ref, b_ref):
    c, sub = HG_CHUNK, HG_SUB
    seq, ctx_len = q_ref.shape[0], vc_ref.shape[0]
    n, nc = seq // c, ctx_len // c
    r = lax.broadcasted_iota(jnp.int32, (c, c), 0)
    s = lax.broadcasted_iota(jnp.int32, (c, c), 1)
    tri = jnp.concatenate([jnp.concatenate([(s <= r).astype(jnp.bfloat16)] * 3, axis=1),
                           jnp.concatenate([(s >= r).astype(jnp.bfloat16)] * 3, axis=1)], axis=0)
    row = lax.broadcasted_iota(jnp.int32, (c, LANES), 0)
    lane = lax.broadcasted_iota(jnp.int32, (c, LANES), 1)
    own = (lane // sub) == (row // sub)
    dmask_f = own & ((lane % sub) <= (row % sub))
    dmask_b = own & ((lane % sub) >= (row % sub))
    pad = jnp.zeros((LANES - c, LANES), MXU_DTYPE)

    heads = [slice(hd * LANES, (hd + 1) * LANES) for hd in range(q_ref.shape[1] // LANES)]
    dirs = ((kff_ref, dmask_f, False), (kfb_ref, dmask_b, True))

    for hd, ln in enumerate(heads):
        sf, sb = jnp.zeros(s_ref.shape[2:], F32), jnp.zeros(s_ref.shape[2:], F32)
        for i in range(nc):
            rf, rb = slice(i * c, (i + 1) * c), slice((nc - 1 - i) * c, (nc - i) * c)
            bf, bb = _cum_decays(lffc_ref[rf, ln], lfbc_ref[rb, ln], tri)
            sf = _hg_state_update(sf, kffc_ref[rf, ln].astype(F32), vc_ref[rf, ln], bf, bf[c - 1:c, :])
            sb = _hg_state_update(sb, kfbc_ref[rb, ln].astype(F32), vc_ref[rb, ln], bb, bb[0:1, :])
        s_ref[hd, 0] = sf
        s_ref[hd, 1] = sb

    def chunk_rows(i, rev):
        return pl.ds(pl.multiple_of(((n - 1 - i) if rev else i) * c, c), c)

    def cum_decays(i):
        return [_cum_decays(lff_ref[chunk_rows(i, False), ln], lfb_ref[chunk_rows(i, True), ln], tri)
                for ln in heads]

    def stage_local(i, bs, slot, exact):
        for hd, ln in enumerate(heads):
            for j, (k_ref, _, rev) in enumerate(dirs):
                rows = chunk_rows(i, rev)
                zcat, qcat, kcat = _hg_local(q_ref[rows, ln].astype(F32), k_ref[rows, ln].astype(F32),
                                             bs[hd][j], rev, exact)
                if exact:
                    z_ref[slot, hd, j * c:(j + 1) * c, :] = zcat
                qc_ref[slot, hd, j, :, 0:qcat.shape[1]] = qcat
                kc_ref[slot, hd, j, 0:c, 0:kcat.shape[1]] = kcat
                b_ref[slot, hd, j] = bs[hd][j]

    def step(i, slot, accumulate, exact):
        off = N_OFF_SEGMENTS * LANES
        nxt = jnp.minimum(i + 1, n - 1)
        b_next = cum_decays(nxt)
        started = []
        for hd, ln in enumerate(heads):
            if exact:
                diag_both = jnp.dot(z_ref[slot, hd], sel_ref[...], preferred_element_type=F32)
            for j, (k_ref, dmask, rev) in enumerate(dirs):
                rows = chunk_rows(i, rev)
                q, k = q_ref[rows, ln].astype(F32), k_ref[rows, ln].astype(F32)
                v, b = v_ref[rows, ln], b_ref[slot, hd, j]
                st = s_ref[hd, j]
                b_end = b[0:1, :] if rev else b[c - 1:c, :]
                below = _mm_nt(qc_ref[slot, hd, j, :, 0:off], kc_ref[slot, hd, j, :, 0:off])
                if exact:
                    diag = diag_both[j * c:(j + 1) * c]
                else:
                    diag = _mm_nt(qc_ref[slot, hd, j, :, off:off + LANES], kc_ref[slot, hd, j, :, off:off + LANES])
                rhs = jnp.concatenate([st.T.astype(MXU_DTYPE), v.astype(MXU_DTYPE), pad], axis=0)
                started.append((hd, ln, j, rows, dmask, (q * jnp.exp2(b)).astype(MXU_DTYPE), rhs, below, diag,
                                _hg_state_update(st, k, v, b, b_end)))
        stage_local(nxt, b_next, 1 - slot, exact)
        for hd, ln, j, rows, dmask, q_dec, rhs, below, diag, st_new in started:
            scores = jnp.where(dmask, diag, 0.0) + below
            o = jnp.dot(jnp.concatenate([q_dec, scores.astype(MXU_DTYPE)], axis=1), rhs,
                        preferred_element_type=F32)
            s_ref[hd, j] = st_new
            if accumulate:
                o_ref[rows, ln] += o
            else:
                o_ref[rows, ln] = o

    def scan(exact):
        stage_local(0, cum_decays(0), 0, exact)

        unroll = HG_UNROLL_EXACT if exact else HG_UNROLL

        def steps(ii, carry, accumulate):
            for u in range(unroll):
                step(ii * unroll + u, u % 2, accumulate, exact)
            return carry

        half_trips = n // (2 * unroll)
        lax.fori_loop(0, half_trips, functools.partial(steps, accumulate=False), 0)
        lax.fori_loop(half_trips, 2 * half_trips, functools.partial(steps, accumulate=True), 0)

    kc_ref[...] = jnp.zeros_like(kc_ref)
    steepest = jnp.minimum(jnp.min(lff_ref[...]), jnp.min(lfb_ref[...]))
    lax.cond(steepest * (sub - 1) >= -HG_SAFE_LOG2, lambda: scan(False), lambda: scan(True))


def _hgrn(hq, hv, lff, kff, lfb, kfb, hvc, lffc, kffc, lfbc, kfbc, batch):
    rows, hw = hq.shape
    seq = rows // batch
    ctx_len = hvc.shape[0] // batch
    heads = hw // HG_EXPAND
    assert seq % (2 * HG_UNROLL * HG_CHUNK) == 0 and ctx_len % HG_CHUNK == 0
    assert HG_UNROLL % HG_UNROLL_EXACT == 0 and HG_UNROLL_EXACT % 2 == 0
    n_seg = N_OFF_SEGMENTS + 1
    src = np.arange(HG_SUB * LANES, dtype=np.int32)[:, None] // LANES
    sel = jnp.asarray(src == np.arange(LANES, dtype=np.int32)[None, :] % HG_SUB, MXU_DTYPE)
    assert heads % HG_HEADS == 0
    width = HG_HEADS * HG_EXPAND
    lat = pl.BlockSpec((seq, width), lambda b, h: (b, h))
    ctx = pl.BlockSpec((ctx_len, width), lambda b, h: (b, h))
    return pl.pallas_call(
        _hgrn_kernel,
        grid=(batch, heads // HG_HEADS),
        in_specs=[lat] * 6 + [ctx] * 5 + [_resident(sel.shape)],
        out_specs=lat,
        out_shape=jax.ShapeDtypeStruct((rows, hw), F32),
        scratch_shapes=[
            pltpu.VMEM((HG_HEADS, 2, HG_EXPAND, HG_EXPAND), F32),
            pltpu.VMEM((2, HG_HEADS, 2 * HG_CHUNK, HG_SUB * LANES), MXU_DTYPE),
            pltpu.VMEM((2, HG_HEADS, 2, HG_CHUNK, n_seg * LANES), MXU_DTYPE),
            pltpu.VMEM((2, HG_HEADS, 2, LANES, n_seg * LANES), MXU_DTYPE),
            pltpu.VMEM((2, HG_HEADS, 2, HG_CHUNK, LANES), F32)],
        compiler_params=_params(2),
        name="hgrn",
    )(hq, hv, lff, kff, lfb, kfb, hvc, lffc, kffc, lfbc, kfbc, sel)


def _tail_kernel(x_ref, mod_ref, pre_ref, post_ref, ghn_ref, oa_ref, oh_ref,
                 wg_ref, woa_ref, woh_ref, wout_ref, win_ref, wo2_ref, o_ref, *, hw, d_ff):
    d = x_ref.shape[1]
    mod = lambda j: mod_ref[j:j + 1, :]
    bf = lambda a: a.astype(MXU_DTYPE)
    halves = [slice(r0, r0 + ROW_HALF) for r0 in range(0, x_ref.shape[0], ROW_HALF)]
    xs = [x_ref[r, :] for r in halves]
    gs = [_mm(bf(_rms(x, pre_ref[1:2, :]) * (1.0 + mod(4)) + mod(3)), wg_ref[...]) for x in xs]
    rs = [bf(_rms(oh_ref[r, :], ghn_ref[...]) * _silu(g[:, :hw])) for r, g in zip(halves, gs)]
    zs = [bf(_sigmoid(g[:, hw:hw + d]) * _mm(oa_ref[r, :], woa_ref[...])
             + _sigmoid(g[:, hw + d:]) * _mm(rr, woh_ref[...])) for r, g, rr in zip(halves, gs, rs)]
    xs = [x + mod(5) * _rms(_mm(z, wout_ref[...]), post_ref[1:2, :]) for x, z in zip(xs, zs)]
    hs = [bf(_rms(x, pre_ref[2:3, :]) * (1.0 + mod(7)) + mod(6)) for x in xs]
    gus = [_mm(h, win_ref[...]) for h in hs]
    acts = [bf(_silu(gu[:, :d_ff]) * gu[:, d_ff:]) for gu in gus]
    ys = [_mm(act, wo2_ref[...]) for act in acts]
    for r, x, y in zip(halves, xs, ys):
        o_ref[r, :] = x + 0.5 * mod(8) * _rms(y, post_ref[2:3, :])


def _tail(x, mod3, tiles_per_batch, pre, post, g_hnorm, oa, oh, wg, woa, woh, wout, w_in, w_out, tm):
    rows, d = x.shape
    assert rows % tm == 0, (rows, tm)
    row = lambda width: pl.BlockSpec((tm, width), lambda i: (i, 0))
    return pl.pallas_call(
        functools.partial(_tail_kernel, hw=g_hnorm.shape[1], d_ff=w_out.shape[0]),
        grid=(rows // tm,),
        in_specs=[row(d),
                  pl.BlockSpec((None, N_MOD, d), lambda i: (i // tiles_per_batch, 0, 0)),
                  _resident(pre.shape), _resident(post.shape), _resident(g_hnorm.shape),
                  row(oa.shape[1]), row(oh.shape[1])]
                 + [_resident(w.shape) for w in (wg, woa, woh, wout, w_in, w_out)],
        out_specs=row(d),
        out_shape=jax.ShapeDtypeStruct((rows, d), F32),
        compiler_params=_params(1),
        name="merge_ffn",
    )(x, mod3, pre, post, g_hnorm, oa, oh, wg, woa, woh, wout, w_in, w_out)


def _rope_tables(seq):
    quarter = ATT_HEAD_DIM // 4
    t = np.arange(seq, dtype=np.int32)
    freqs = np.float32(ROPE_BASE) ** (-np.arange(quarter, dtype=np.float32) / np.float32(quarter))
    cos, sin = [], []
    for pos in (t // GRID_W, t % GRID_W):
        ang = pos.astype(np.float32)[:, None] * freqs[None, :]
        cos += [np.cos(ang), np.cos(ang)]
        sin += [-np.sin(ang), np.sin(ang)]
    reps = LANES // ATT_HEAD_DIM
    return (jnp.asarray(np.concatenate(cos * reps, axis=1), F32),
            jnp.asarray(np.concatenate(sin * reps, axis=1), F32))


def _pair_heads(a, axis):
    heads = jnp.split(a, ATT_HEADS, axis=axis)
    return jnp.concatenate([heads[h] for m in range(ATT_GROUP) for h in (m, ATT_GROUP + m)], axis=axis)


def kernel(x, c, ctx, c_ctx, w_ada, b_ada, norm_pre, norm_post, ffn1_w_in, ffn1_w_out, ffn2_w_in, ffn2_w_out,
           mix_w_in, attn_sink, hgrn_lb_fwd, hgrn_lb_bwd, hgrn_norm, w_o_attn, w_o_hgrn, w_out):
    batch, seq, d = x.shape
    ctx_len = ctx.shape[1]
    depth = w_ada.shape[0]
    assert depth == 1, "single-layer problem: the context stream is only read, never written back"
    hw = hgrn_norm.shape[1]
    tm_ffn, tm_in, tm_tail = FFN_PARTS * ROW_HALF, INPROJ_PARTS * ROW_HALF, TAIL_PARTS * ROW_HALF
    bf = lambda a: a.astype(MXU_DTYPE)

    mod_rows = -(-(batch + 1) // 8) * 8
    cs = jnp.concatenate([c, c_ctx[None, :], jnp.zeros((mod_rows - batch - 1, d), F32)], axis=0)
    mod3 = _modulation(cs, w_ada[0], b_ada[0]).reshape(mod_rows, N_MOD, d)
    ctx_row = batch

    xl = x.reshape(batch * seq, d)
    xc = ctx.reshape(batch * ctx_len, d)
    pre = [norm_pre[0, j][None, :] for j in range(3)]
    post = [norm_post[0, j][None, :] for j in range(3)]

    w1_in, w1_out = bf(ffn1_w_in[0]), bf(ffn1_w_out[0])
    xl = _ffn(xl, mod3, lambda i: i // (seq // tm_ffn), 0, pre[0], post[0], w1_in, w1_out, tm_ffn)
    xc = _ffn(xc, mod3, lambda i: ctx_row, 0, pre[0], post[0], w1_in, w1_out, tm_ffn)

    wm = mix_w_in[0]
    kv0 = ATT_WIDTH
    hg0 = kv0 + 2 * ATT_KV_WIDTH
    gate0 = hg0 + 4 * hw
    w_lat = bf(jnp.concatenate([_pair_heads(wm[:, :kv0], 1), wm[:, kv0:gate0]], axis=1))
    w_ctx = bf(jnp.concatenate([wm[:, kv0:hg0], wm[:, hg0 + hw:gate0]], axis=1))
    cos, sin = _rope_tables(seq)
    q, kv, hq, lff, kff, lfb, kfb, hv = _inproj(
        xl, mod3, seq // tm_in, pre[1], cos, sin, hgrn_lb_fwd, hgrn_lb_bwd, w_lat, tm_in)
    kvc, lffc, kffc, lfbc, kfbc, hvc = _inproj_ctx(xc, mod3, ctx_row, pre[1], hgrn_lb_fwd, hgrn_lb_bwd, w_ctx,
                                                   ROW_HALF)

    o_att = _attention(attn_sink[0], q, kv, kvc, batch)
    o_hg = _hgrn(hq, hv, lff, kff, lfb, kfb, hvc, lffc, kffc, lfbc, kfbc, batch)

    woa = bf(_pair_heads(w_o_attn[0], 0))
    xl = _tail(xl, mod3, seq // tm_tail, norm_pre[0], norm_post[0], hgrn_norm, o_att, o_hg,
               bf(wm[:, gate0:]), woa, bf(w_o_hgrn[0]), bf(w_out[0]), bf(ffn2_w_in[0]), bf(ffn2_w_out[0]), tm_tail)
    return xl.reshape(batch, seq, d)
```

```python
import functools

import jax
import jax.numpy as jnp
import numpy as np
from jax import lax
from jax.experimental import pallas as pl
from jax.experimental.pallas import tpu as pltpu

F32 = jnp.float32
MXU_DTYPE = jnp.bfloat16

N_MOD = 9
GRID_W = 64
ATT_HEADS = 8
ATT_KV_HEADS = 2
ATT_GROUP = ATT_HEADS // ATT_KV_HEADS
ATT_HEAD_DIM = 64
ATT_WIDTH = ATT_HEADS * ATT_HEAD_DIM
ATT_KV_WIDTH = ATT_KV_HEADS * ATT_HEAD_DIM
WINDOW = 128
ATT_BLOCK = 128
ATT_SLABS = 1
ATT_QBLOCKS = 8
ATT_LOOKAHEAD = 3
ROPE_BASE = 10000.0
HG_EXPAND = 128
ROW_HALF = 256
FFN_PARTS = 4
INPROJ_PARTS = 4
TAIL_PARTS = 2
EPS = 1e-6
NEG_INF = -1e30
LOG2E = 1.4426950408889634

LANES = 128
HG_CHUNK = 64
HG_SUB = 16
HG_HEADS = 4
HG_UNROLL = 4
HG_UNROLL_EXACT = 2
N_OFF_SEGMENTS = 3
HG_SAFE_LOG2 = 120.0
VMEM_LIMIT = 56 * 1024 * 1024


def _sigmoid(x):
    return 1.0 / (1.0 + jnp.exp(-x))


def _silu(x):
    return x * _sigmoid(x)


def _rms(x, g):
    ms = jnp.mean(x * x, axis=-1, keepdims=True)
    return x * lax.rsqrt(ms + EPS) * g


def _mm(a, b):
    return jnp.dot(a.astype(MXU_DTYPE), b.astype(MXU_DTYPE), preferred_element_type=F32)


def _mm_nt(a, b):
    return lax.dot_general(a.astype(MXU_DTYPE), b.astype(MXU_DTYPE), (((1,), (1,)), ((), ())),
                           preferred_element_type=F32)


def _mm_tn(a, b):
    return lax.dot_general(a.astype(MXU_DTYPE), b.astype(MXU_DTYPE), (((0,), (0,)), ((), ())),
                           preferred_element_type=F32)


def _resident(shape):
    nd = len(shape)
    return pl.BlockSpec(shape, lambda *_: (0,) * nd, pipeline_mode=pl.Buffered(1))


def _row_out(rows, tm, width, pair):
    if pair:
        return (pl.BlockSpec((2, tm, width), lambda i: (0, i, 0)),
                jax.ShapeDtypeStruct((2, rows, width), jnp.bfloat16))
    return pl.BlockSpec((tm, width), lambda i: (i, 0)), jax.ShapeDtypeStruct((rows, width), MXU_DTYPE)


def _params(n_axes):
    return pltpu.CompilerParams(dimension_semantics=("arbitrary",) * n_axes,
                                vmem_limit_bytes=VMEM_LIMIT)


def _mod_kernel(c_ref, w_ref, b_ref, o_ref):
    o_ref[...] = _mm(_silu(c_ref[...]), w_ref[...]) + b_ref[...]


def _modulation(cs, w_ada, b_ada):
    rows, d = cs.shape
    n = w_ada.shape[1]
    tn = n // 8
    return pl.pallas_call(
        _mod_kernel,
        grid=(n // tn,),
        in_specs=[pl.BlockSpec((rows, d), lambda j: (0, 0)),
                  pl.BlockSpec((d, tn), lambda j: (0, j)),
                  pl.BlockSpec((1, tn), lambda j: (0, j))],
        out_specs=pl.BlockSpec((rows, tn), lambda j: (0, j)),
        out_shape=jax.ShapeDtypeStruct((rows, n), F32),
        compiler_params=_params(1),
        name="modulation",
    )(cs, w_ada, b_ada.reshape(1, n))


def _ffn_kernel(x_ref, mod_ref, gpre_ref, gpost_ref, win_ref, wout_ref, o_ref, *, j0, d_ff):
    shift, scale, gate = mod_ref[j0:j0 + 1, :], mod_ref[j0 + 1:j0 + 2, :], mod_ref[j0 + 2:j0 + 3, :]
    halves = [slice(r0, r0 + ROW_HALF) for r0 in range(0, x_ref.shape[0], ROW_HALF)]
    xs = [x_ref[r, :] for r in halves]
    hs = [(_rms(x, gpre_ref[...]) * (1.0 + scale) + shift).astype(MXU_DTYPE) for x in xs]
    gus = [_mm(h, win_ref[...]) for h in hs]
    acts = [(_silu(gu[:, :d_ff]) * gu[:, d_ff:]).astype(MXU_DTYPE) for gu in gus]
    ys = [_mm(act, wout_ref[...]) for act in acts]
    for r, x, y in zip(halves, xs, ys):
        o_ref[r, :] = x + 0.5 * gate * _rms(y, gpost_ref[...])


def _ffn(x, mod3, mod_row, j0, g_pre, g_post, w_in, w_out, tm):
    rows, d = x.shape
    assert rows % tm == 0, (rows, tm)
    d_ff = w_out.shape[0]
    return pl.pallas_call(
        functools.partial(_ffn_kernel, j0=j0, d_ff=d_ff),
        grid=(rows // tm,),
        in_specs=[pl.BlockSpec((tm, d), lambda i: (i, 0)),
                  pl.BlockSpec((None, N_MOD, d), lambda i: (mod_row(i), 0, 0)),
                  _resident((1, d)), _resident((1, d)),
                  _resident(w_in.shape), _resident(w_out.shape)],
        out_specs=pl.BlockSpec((tm, d), lambda i: (i, 0)),
        out_shape=jax.ShapeDtypeStruct((rows, d), F32),
        compiler_params=_params(1),
        name="ffn",
    )(x, mod3, g_pre, g_post, w_in, w_out)


def _lower_bound(lb_ref):
    a = lb_ref[...]
    e = jnp.exp(a - jnp.max(a, axis=0, keepdims=True))
    return e[0:1, :] / jnp.sum(e, axis=0, keepdims=True)


def _forget(z, lb):
    f = lb + (1.0 - lb) * _sigmoid(z)
    return jnp.log2(f), 1.0 - f


def _rope(x, cos, sin):
    w = x.shape[1]
    reps = w // LANES
    cos_w = jnp.concatenate([cos] * reps, axis=1) if reps > 1 else cos
    sin_w = jnp.concatenate([sin] * reps, axis=1) if reps > 1 else sin
    lane = lax.broadcasted_iota(jnp.int32, x.shape, 1)
    quarter = ATT_HEAD_DIM // 4
    first = (lane % (2 * quarter)) < quarter
    partner = jnp.where(first, pltpu.roll(x, w - quarter, 1), pltpu.roll(x, quarter, 1))
    return x * cos_w + partner * sin_w


def _inproj_kernel(x_ref, mod_ref, gpre_ref, cos_ref, sin_ref, lbf_ref, lbb_ref, w_ref,
                   q_ref, kv_ref, hq_ref, lff_ref, kff_ref, lfb_ref, kfb_ref, hv_ref, *, hw):
    shift, scale = mod_ref[3:4, :], mod_ref[4:5, :]
    halves = [slice(r0, r0 + ROW_HALF) for r0 in range(0, x_ref.shape[0], ROW_HALF)]
    hs = [(_rms(x_ref[r, :], gpre_ref[...]) * (1.0 + scale) + shift).astype(MXU_DTYPE) for r in halves]
    c0 = 0

    def proj(width):
        nonlocal c0
        outs = [jnp.dot(h, w_ref[:, c0:c0 + width], preferred_element_type=F32) for h in hs]
        c0 += width
        return zip(halves, outs)

    for r, q in proj(ATT_WIDTH):
        q_ref[r, :] = (_rope(q, cos_ref[r, :], sin_ref[r, :]) * (LOG2E * ATT_HEAD_DIM ** -0.5)).astype(q_ref.dtype)
    for r, kv in proj(2 * ATT_KV_WIDTH):
        kv_ref[r, :ATT_KV_WIDTH] = _rope(kv[:, :ATT_KV_WIDTH], cos_ref[r, :], sin_ref[r, :]).astype(kv_ref.dtype)
        kv_ref[r, ATT_KV_WIDTH:] = kv[:, ATT_KV_WIDTH:].astype(kv_ref.dtype)
    for r, z in proj(hw):
        hq_ref[r, :] = _silu(z).astype(hq_ref.dtype)
    for lb_ref, lf_ref, kf_ref in ((lbf_ref, lff_ref, kff_ref), (lbb_ref, lfb_ref, kfb_ref)):
        for r, z in proj(hw):
            lf, kf = _forget(z, _lower_bound(lb_ref))
            lf_ref[0, r, :], lf_ref[1, r, :] = _split_log_decay(lf)
            kf_ref[r, :] = kf.astype(kf_ref.dtype)
    for r, z in proj(hw):
        hv_ref[r, :] = z.astype(hv_ref.dtype)


def _inproj(x, mod3, tiles_per_batch, g_pre, cos, sin, lb_f, lb_b, w, tm):
    rows, d = x.shape
    assert rows % tm == 0, (rows, tm)
    hw = lb_f.shape[1]
    row = lambda width: pl.BlockSpec((tm, width), lambda i: (i, 0))
    outs = [_row_out(rows, tm, wd, pair) for wd, pair in (
        (ATT_WIDTH, False), (2 * ATT_KV_WIDTH, False), (hw, False), (hw, True), (hw, False), (hw, True),
        (hw, False), (hw, False))]
    return pl.pallas_call(
        functools.partial(_inproj_kernel, hw=hw),
        grid=(rows // tm,),
        in_specs=[row(d),
                  pl.BlockSpec((None, N_MOD, d), lambda i: (i // tiles_per_batch, 0, 0)),
                  _resident((1, d)),
                  pl.BlockSpec((tm, LANES), lambda i: (i % tiles_per_batch, 0)),
                  pl.BlockSpec((tm, LANES), lambda i: (i % tiles_per_batch, 0)),
                  _resident(lb_f.shape), _resident(lb_b.shape), _resident(w.shape)],
        out_specs=[o[0] for o in outs],
        out_shape=[o[1] for o in outs],
        compiler_params=_params(1),
        name="inproj",
    )(x, mod3, g_pre, cos, sin, lb_f, lb_b, w)


def _inproj_ctx_kernel(x_ref, mod_ref, gpre_ref, lbf_ref, lbb_ref, w_ref,
                       kv_ref, lff_ref, kff_ref, lfb_ref, kfb_ref, hv_ref, *, hw):
    x = x_ref[...]
    shift, scale = mod_ref[3:4, :], mod_ref[4:5, :]
    h = (_rms(x, gpre_ref[...]) * (1.0 + scale) + shift).astype(MXU_DTYPE)
    kvw = 2 * ATT_KV_WIDTH
    kv_ref[...] = jnp.dot(h, w_ref[:, :kvw], preferred_element_type=F32).astype(kv_ref.dtype)
    lf, kf = _forget(jnp.dot(h, w_ref[:, kvw:kvw + hw], preferred_element_type=F32), _lower_bound(lbf_ref))
    lff_ref[0], lff_ref[1] = _split_log_decay(lf)
    kff_ref[...] = kf.astype(kff_ref.dtype)
    lf, kf = _forget(jnp.dot(h, w_ref[:, kvw + hw:kvw + 2 * hw], preferred_element_type=F32),
                     _lower_bound(lbb_ref))
    lfb_ref[0], lfb_ref[1] = _split_log_decay(lf)
    kfb_ref[...] = kf.astype(kfb_ref.dtype)
    hv_ref[...] = jnp.dot(h, w_ref[:, kvw + 2 * hw:], preferred_element_type=F32).astype(hv_ref.dtype)


def _inproj_ctx(xc, mod3, ctx_row, g_pre, lb_f, lb_b, w, tm):
    rows, d = xc.shape
    assert rows % tm == 0, (rows, tm)
    hw = lb_f.shape[1]
    row = lambda width: pl.BlockSpec((tm, width), lambda i: (i, 0))
    outs = [_row_out(rows, tm, wd, pair) for wd, pair in (
        (2 * ATT_KV_WIDTH, False), (hw, True), (hw, False), (hw, True), (hw, False), (hw, False))]
    return pl.pallas_call(
        functools.partial(_inproj_ctx_kernel, hw=hw),
        grid=(rows // tm,),
        in_specs=[row(d),
                  pl.BlockSpec((None, N_MOD, d), lambda i: (ctx_row, 0, 0)),
                  _resident((1, d)), _resident(lb_f.shape), _resident(lb_b.shape), _resident(w.shape)],
        out_specs=[o[0] for o in outs],
        out_shape=[o[1] for o in outs],
        compiler_params=_params(1),
        name="inproj_ctx",
    )(xc, mod3, g_pre, lb_f, lb_b, w)


def _attn_kernel(sink_ref, q_ref, kv_ref, kvc_ref, o_ref, *, seq, band):
    step = pl.program_id(1)
    kvc = kvc_ref[...]
    low_lane = lax.broadcasted_iota(jnp.int32, (ATT_BLOCK, LANES), 1) < ATT_HEAD_DIM
    low_row = lax.broadcasted_iota(jnp.int32, (LANES, ATT_BLOCK), 0) < ATT_HEAD_DIM
    heads_per_group = 2 * ATT_SLABS

    def block_keys(qb):
        n = step * ATT_QBLOCKS + qb
        start = pl.multiple_of(jnp.clip(n * ATT_BLOCK - ATT_BLOCK, 0, seq - band), ATT_BLOCK)
        kvb = kv_ref[pl.ds(start, band), :]
        k_all = jnp.concatenate([kvb[:, :ATT_KV_WIDTH], kvc[:, :ATT_KV_WIDTH]], axis=0)
        v_t = jnp.concatenate([kvb[:, ATT_KV_WIDTH:], kvc[:, ATT_KV_WIDTH:]], axis=0).T
        kpos = start + lax.broadcasted_iota(jnp.int32, (band, ATT_BLOCK), 0)
        qpos = n * ATT_BLOCK + lax.broadcasted_iota(jnp.int32, (band, ATT_BLOCK), 1)
        valid = jnp.abs(kpos - qpos) <= WINDOW
        return k_all, v_t, jnp.concatenate([valid] * heads_per_group, axis=1)

    keys = [block_keys(qb) for qb in range(ATT_QBLOCKS)]

    def unit_scores(qb, m0):
        q_rows, sink_row = [], []
        for m in range(m0, m0 + ATT_SLABS):
            qm = q_ref[qb * ATT_BLOCK:(qb + 1) * ATT_BLOCK, m * LANES:(m + 1) * LANES]
            zero = jnp.zeros_like(qm)
            q_rows += [jnp.where(low_lane, qm, zero), jnp.where(low_lane, zero, qm)]
            sink_row += [jnp.full((1, ATT_BLOCK), sink_ref[m] * LOG2E, F32),
                         jnp.full((1, ATT_BLOCK), sink_ref[ATT_GROUP + m] * LOG2E, F32)]
        return _mm_nt(keys[qb][0], jnp.concatenate(q_rows, axis=0)), jnp.concatenate(sink_row, axis=1)

    units = [(qb, m0) for qb in range(ATT_QBLOCKS) for m0 in range(0, ATT_GROUP, ATT_SLABS)]

    def finish(o_t, den, qb, m0):
        o_t = o_t * (1.0 / den)
        for j, m in enumerate(range(m0, m0 + ATT_SLABS)):
            o_lo = o_t[:, (2 * j) * ATT_BLOCK:(2 * j + 1) * ATT_BLOCK]
            o_hi = o_t[:, (2 * j + 1) * ATT_BLOCK:(2 * j + 2) * ATT_BLOCK]
            o_ref[qb * ATT_BLOCK:(qb + 1) * ATT_BLOCK, m * LANES:(m + 1) * LANES] = (
                jnp.where(low_row, o_lo, o_hi).T.astype(o_ref.dtype))

    pending = [unit_scores(*unit) for unit in units[:ATT_LOOKAHEAD]]
    unfinished = None
    for u, (qb, m0) in enumerate(units):
        if u + ATT_LOOKAHEAD < len(units):
            pending.append(unit_scores(*units[u + ATT_LOOKAHEAD]))
        s_t, sink2 = pending[u]
        _, v_t, valid = keys[qb]
        s_t = jnp.concatenate([jnp.where(valid, s_t[:band], NEG_INF), s_t[band:]], axis=0)
        m = jnp.maximum(jnp.max(s_t, axis=0, keepdims=True), sink2)
        e = jnp.exp2(s_t - m)
        den = jnp.sum(e, axis=0, keepdims=True) + jnp.exp2(sink2 - m)
        o_t = _mm(v_t, e)
        if unfinished is not None:
            finish(*unfinished)
        unfinished = (o_t, den, qb, m0)
    finish(*unfinished)


def _attention(sink, q, kv, kvc, batch):
    rows, _ = q.shape
    seq = rows // batch
    ctx_len = kvc.shape[0] // batch
    tq = ATT_QBLOCKS * ATT_BLOCK
    steps = seq // tq
    band = 3 * ATT_BLOCK
    return pl.pallas_call(
        functools.partial(_attn_kernel, seq=seq, band=band),
        grid=(batch, steps),
        in_specs=[pl.BlockSpec(memory_space=pltpu.SMEM),
                  pl.BlockSpec((tq, ATT_WIDTH), lambda b, n: (b * steps + n, 0)),
                  pl.BlockSpec((seq, 2 * ATT_KV_WIDTH), lambda b, n: (b, 0)),
                  pl.BlockSpec((ctx_len, 2 * ATT_KV_WIDTH), lambda b, n: (b, 0))],
        out_specs=pl.BlockSpec((tq, ATT_WIDTH), lambda b, n: (b * steps + n, 0)),
        out_shape=jax.ShapeDtypeStruct((rows, ATT_WIDTH), MXU_DTYPE),
        compiler_params=_params(2),
        name="attention",
    )(sink, q, kv, kvc)


def _split_log_decay(lf):
    hi = lf.astype(jnp.bfloat16)
    lo = (lf - hi.astype(F32)).astype(jnp.bfloat16)
    return hi, lo


def _cum_decays(lf_fwd, lf_bwd, tri):
    c = lf_fwd.shape[1]
    parts = jnp.concatenate([jnp.concatenate([lf_fwd[p], lf_bwd[p]], axis=1) for p in range(2)], axis=0)
    both = jnp.dot(tri, parts, preferred_element_type=F32)
    return [both[:c, :LANES], both[c:, LANES:]]


def _hg_state_update(st, k, v, b, b_end):
    k_dec = k * jnp.exp2(b_end - b)
    return st * jnp.exp2(b_end) + _mm_tn(v, k_dec)


def _hg_diag_tiles(q4, k4, b4, rev):
    nblk, sub, _ = q4.shape
    half = sub // 2
    tiles = []
    for s in range(sub):
        ks, bs = k4[:, s:s + 1, :], b4[:, s:s + 1, :]
        halves = []
        for hf in range(2):
            needed = (hf == 0 or s >= half) if rev else (hf == 1 or s < half)
            if needed:
                rows = slice(hf * half, (hf + 1) * half)
                halves.append(q4[:, rows] * ks * jnp.exp2(jnp.minimum(b4[:, rows] - bs, 0.0)))
            else:
                halves.append(jnp.zeros((nblk, half, LANES), F32))
        tiles.append(jnp.concatenate(halves, axis=1).reshape(nblk * sub, LANES).astype(MXU_DTYPE))
    return jnp.concatenate(tiles, axis=1)


def _hg_local(q, k, b, rev, exact):
    c, sub = HG_CHUNK, HG_SUB
    nblk = c // sub
    q4, k4, b4 = (a.reshape(nblk, sub, LANES) for a in (q, k, b))
    zero = jnp.zeros((sub, LANES), F32)

    def factors(q_blocks, k_blocks, anchor):
        ba = b[anchor:anchor + 1]
        qs = [q4[i] * jnp.exp2(b4[i] - ba) if i in q_blocks else zero for i in range(nblk)]
        ks = [k4[i] * jnp.exp2(ba - b4[i]) if i in k_blocks else zero for i in range(nblk)]
        return jnp.concatenate(qs, axis=0), jnp.concatenate(ks, axis=0)

    if rev:
        pairs = [factors((0,), (1,), sub), factors((2,), (3,), 3 * sub), factors((0, 1), (2, 3), 2 * sub)]
    else:
        pairs = [factors((1,), (0,), sub - 1), factors((3,), (2,), 3 * sub - 1), factors((2, 3), (0, 1), 2 * sub - 1)]
    if not exact:
        anchors = [b4[i, (sub - 1 if rev else 0):(sub if rev else 1), :] for i in range(nblk)]
        pairs.append((jnp.concatenate([q4[i] * jnp.exp2(b4[i] - anchors[i]) for i in range(nblk)], axis=0),
                      jnp.concatenate([k4[i] * jnp.exp2(anchors[i] - b4[i]) for i in range(nblk)], axis=0)))
    qcat = jnp.concatenate([p[0] for p in pairs], axis=1).astype(MXU_DTYPE)
    kcat = jnp.concatenate([p[1] for p in pairs], axis=1)
    kcat_t = jnp.concatenate([kcat, jnp.zeros((LANES - c, kcat.shape[1]), F32)], axis=0).T.astype(MXU_DTYPE)
    return (_hg_diag_tiles(q4, k4, b4, rev) if exact else None), qcat, kcat_t


def _hgrn_kernel(q_ref, v_ref, lff_ref, kff_ref, lfb_ref, kfb_ref,
                 vc_ref, lffc_ref, kffc_ref, lfbc_ref, kfbc_ref, sel_ref, o_ref,
                 s_ref, z_ref, qc_ref, kc_ref, b_ref):
    c, sub = HG_CHUNK, HG_SUB
    seq, ctx_len = q_ref.shape[0], vc_ref.shape[0]
    n, nc = seq // c, ctx_len // c
    r = lax.broadcasted_iota(jnp.int32, (c, c), 0)
    s = lax.broadcasted_iota(jnp.int32, (c, c), 1)
    tri = jnp.concatenate([jnp.concatenate([(s <= r).astype(jnp.bfloat16)] * 2, axis=1),
                           jnp.concatenate([(s >= r).astype(jnp.bfloat16)] * 2, axis=1)], axis=0)
    row = lax.broadcasted_iota(jnp.int32, (c, LANES), 0)
    lane = lax.broadcasted_iota(jnp.int32, (c, LANES), 1)
    own = (lane // sub) == (row // sub)
    dmask_f = own & ((lane % sub) <= (row % sub))
    dmask_b = own & ((lane % sub) >= (row % sub))
    pad = jnp.zeros((LANES - c, LANES), MXU_DTYPE)

    heads = [slice(hd * LANES, (hd + 1) * LANES) for hd in range(q_ref.shape[1] // LANES)]
    dirs = ((kff_ref, dmask_f, False), (kfb_ref, dmask_b, True))

    for hd, ln in enumerate(heads):
        sf, sb = jnp.zeros(s_ref.shape[2:], F32), jnp.zeros(s_ref.shape[2:], F32)
        for i in range(nc):
            rf, rb = slice(i * c, (i + 1) * c), slice((nc - 1 - i) * c, (nc - i) * c)
            bf, bb = _cum_decays(lffc_ref[:, rf, ln], lfbc_ref[:, rb, ln], tri)
            sf = _hg_state_update(sf, kffc_ref[rf, ln].astype(F32), vc_ref[rf, ln], bf, bf[c - 1:c, :])
            sb = _hg_state_update(sb, kfbc_ref[rb, ln].astype(F32), vc_ref[rb, ln], bb, bb[0:1, :])
        s_ref[hd, 0] = sf
        s_ref[hd, 1] = sb

    def chunk_rows(i, rev):
        return pl.ds(pl.multiple_of(((n - 1 - i) if rev else i) * c, c), c)

    def cum_decays(i):
        return [_cum_decays(lff_ref[:, chunk_rows(i, False), ln], lfb_ref[:, chunk_rows(i, True), ln], tri)
                for ln in heads]

    def stage_local(i, bs, slot, exact):
        for hd, ln in enumerate(heads):
            for j, (k_ref, _, rev) in enumerate(dirs):
                rows = chunk_rows(i, rev)
                q, k = q_ref[rows, ln].astype(F32), k_ref[rows, ln].astype(F32)
                zcat, qcat, kcat = _hg_local(q, k, bs[hd][j], rev, exact)
                if exact:
                    z_ref[slot, hd, j * c:(j + 1) * c, :] = zcat
                qc_ref[slot, hd, j, :, 0:qcat.shape[1]] = qcat
                kc_ref[slot, hd, j, 0:kcat.shape[0], :] = kcat
                b_ref[slot, hd, j] = bs[hd][j]

    def step(i, slot, accumulate, exact):
        off = N_OFF_SEGMENTS * LANES
        nxt = jnp.minimum(i + 1, n - 1)
        b_next = cum_decays(nxt)
        started = []
        for hd, ln in enumerate(heads):
            if exact:
                diag_both = jnp.dot(z_ref[slot, hd], sel_ref[...], preferred_element_type=F32)
            for j, (k_ref, dmask, rev) in enumerate(dirs):
                rows = chunk_rows(i, rev)
                q, k = q_ref[rows, ln].astype(F32), k_ref[rows, ln].astype(F32)
                v, b = v_ref[rows, ln], b_ref[slot, hd, j]
                st = s_ref[hd, j]
                b_end = b[0:1, :] if rev else b[c - 1:c, :]
                below = jnp.dot(qc_ref[slot, hd, j, :, 0:off], kc_ref[slot, hd, j, 0:off, :],
                                preferred_element_type=F32)
                if exact:
                    diag = diag_both[j * c:(j + 1) * c]
                else:
                    diag = jnp.dot(qc_ref[slot, hd, j, :, off:off + LANES], kc_ref[slot, hd, j, off:off + LANES, :],
                                   preferred_element_type=F32)
                rhs = jnp.concatenate([st.T.astype(MXU_DTYPE), v.astype(MXU_DTYPE), pad], axis=0)
                started.append((hd, ln, j, rows, dmask, (q * jnp.exp2(b)).astype(MXU_DTYPE), rhs, below, diag,
                                _hg_state_update(st, k, v, b, b_end)))
        stage_local(nxt, b_next, 1 - slot, exact)
        for hd, ln, j, rows, dmask, q_dec, rhs, below, diag, st_new in started:
            scores = jnp.where(dmask, diag, 0.0) + below
            o = jnp.dot(jnp.concatenate([q_dec, scores.astype(MXU_DTYPE)], axis=1), rhs,
                        preferred_element_type=F32)
            s_ref[hd, j] = st_new
            if accumulate:
                o_ref[rows, ln] += o
            else:
                o_ref[rows, ln] = o

    def scan(exact):
        stage_local(0, cum_decays(0), 0, exact)

        unroll = HG_UNROLL_EXACT if exact else HG_UNROLL

        def steps(ii, carry, accumulate):
            for u in range(unroll):
                step(ii * unroll + u, u % 2, accumulate, exact)
            return carry

        half_trips = n // (2 * unroll)
        lax.fori_loop(0, half_trips, functools.partial(steps, accumulate=False), 0)
        lax.fori_loop(half_trips, 2 * half_trips, functools.partial(steps, accumulate=True), 0)

    steepest = jnp.minimum(jnp.min(lff_ref[0].astype(F32)), jnp.min(lfb_ref[0].astype(F32)))
    lax.cond(steepest * (sub - 1) >= -HG_SAFE_LOG2, lambda: scan(False), lambda: scan(True))


def _hgrn(hq, hv, lff, kff, lfb, kfb, hvc, lffc, kffc, lfbc, kfbc, batch):
    rows, hw = hq.shape
    seq = rows // batch
    ctx_len = hvc.shape[0] // batch
    heads = hw // HG_EXPAND
    assert seq % (2 * HG_UNROLL * HG_CHUNK) == 0 and ctx_len % HG_CHUNK == 0
    assert HG_UNROLL % HG_UNROLL_EXACT == 0 and HG_UNROLL_EXACT % 2 == 0
    n_seg = N_OFF_SEGMENTS + 1
    src = np.arange(HG_SUB * LANES, dtype=np.int32)[:, None] // LANES
    sel = jnp.asarray(src == np.arange(LANES, dtype=np.int32)[None, :] % HG_SUB, MXU_DTYPE)
    assert heads % HG_HEADS == 0
    width = HG_HEADS * HG_EXPAND
    lat = pl.BlockSpec((seq, width), lambda b, h: (b, h))
    ctx = pl.BlockSpec((ctx_len, width), lambda b, h: (b, h))
    lat2 = pl.BlockSpec((2, seq, width), lambda b, h: (0, b, h))
    ctx2 = pl.BlockSpec((2, ctx_len, width), lambda b, h: (0, b, h))
    return pl.pallas_call(
        _hgrn_kernel,
        grid=(batch, heads // HG_HEADS),
        in_specs=[lat, lat, lat2, lat, lat2, lat, ctx, ctx2, ctx, ctx2, ctx, _resident(sel.shape)],
        out_specs=lat,
        out_shape=jax.ShapeDtypeStruct((rows, hw), F32),
        scratch_shapes=[
            pltpu.VMEM((HG_HEADS, 2, HG_EXPAND, HG_EXPAND), F32),
            pltpu.VMEM((2, HG_HEADS, 2 * HG_CHUNK, HG_SUB * LANES), MXU_DTYPE),
            pltpu.VMEM((2, HG_HEADS, 2, HG_CHUNK, n_seg * LANES), MXU_DTYPE),
            pltpu.VMEM((2, HG_HEADS, 2, n_seg * LANES, LANES), MXU_DTYPE),
            pltpu.VMEM((2, HG_HEADS, 2, HG_CHUNK, LANES), F32)],
        compiler_params=_params(2),
        name="hgrn",
    )(hq, hv, lff, kff, lfb, kfb, hvc, lffc, kffc, lfbc, kfbc, sel)


def _tail_kernel(x_ref, mod_ref, pre_ref, post_ref, ghn_ref, oa_ref, oh_ref,
                 wg_ref, woa_ref, woh_ref, wout_ref, win_ref, wo2_ref, o_ref, *, hw, d_ff):
    d = x_ref.shape[1]
    mod = lambda j: mod_ref[j:j + 1, :]
    bf = lambda a: a.astype(MXU_DTYPE)
    halves = [slice(r0, r0 + ROW_HALF) for r0 in range(0, x_ref.shape[0], ROW_HALF)]
    xs = [x_ref[r, :] for r in halves]
    gs = [_mm(bf(_rms(x, pre_ref[1:2, :]) * (1.0 + mod(4)) + mod(3)), wg_ref[...]) for x in xs]
    rs = [bf(_rms(oh_ref[r, :], ghn_ref[...]) * _silu(g[:, :hw])) for r, g in zip(halves, gs)]
    zs = [bf(_sigmoid(g[:, hw:hw + d]) * _mm(oa_ref[r, :], woa_ref[...])
             + _sigmoid(g[:, hw + d:]) * _mm(rr, woh_ref[...])) for r, g, rr in zip(halves, gs, rs)]
    xs = [x + mod(5) * _rms(_mm(z, wout_ref[...]), post_ref[1:2, :]) for x, z in zip(xs, zs)]
    hs = [bf(_rms(x, pre_ref[2:3, :]) * (1.0 + mod(7)) + mod(6)) for x in xs]
    gus = [_mm(h, win_ref[...]) for h in hs]
    acts = [bf(_silu(gu[:, :d_ff]) * gu[:, d_ff:]) for gu in gus]
    ys = [_mm(act, wo2_ref[...]) for act in acts]
    for r, x, y in zip(halves, xs, ys):
        o_ref[r, :] = x + 0.5 * mod(8) * _rms(y, post_ref[2:3, :])


def _tail(x, mod3, tiles_per_batch, pre, post, g_hnorm, oa, oh, wg, woa, woh, wout, w_in, w_out, tm):
    rows, d = x.shape
    assert rows % tm == 0, (rows, tm)
    row = lambda width: pl.BlockSpec((tm, width), lambda i: (i, 0))
    return pl.pallas_call(
        functools.partial(_tail_kernel, hw=g_hnorm.shape[1], d_ff=w_out.shape[0]),
        grid=(rows // tm,),
        in_specs=[row(d),
                  pl.BlockSpec((None, N_MOD, d), lambda i: (i // tiles_per_batch, 0, 0)),
                  _resident(pre.shape), _resident(post.shape), _resident(g_hnorm.shape),
                  row(oa.shape[1]), row(oh.shape[1])]
                 + [_resident(w.shape) for w in (wg, woa, woh, wout, w_in, w_out)],
        out_specs=row(d),
        out_shape=jax.ShapeDtypeStruct((rows, d), F32),
        compiler_params=_params(1),
        name="merge_ffn",
    )(x, mod3, pre, post, g_hnorm, oa, oh, wg, woa, woh, wout, w_in, w_out)


def _rope_tables(seq):
    quarter = ATT_HEAD_DIM // 4
    t = np.arange(seq, dtype=np.int32)
    freqs = np.float32(ROPE_BASE) ** (-np.arange(quarter, dtype=np.float32) / np.float32(quarter))
    cos, sin = [], []
    for pos in (t // GRID_W, t % GRID_W):
        ang = pos.astype(np.float32)[:, None] * freqs[None, :]
        cos += [np.cos(ang), np.cos(ang)]
        sin += [-np.sin(ang), np.sin(ang)]
    reps = LANES // ATT_HEAD_DIM
    return (jnp.asarray(np.concatenate(cos * reps, axis=1), F32),
            jnp.asarray(np.concatenate(sin * reps, axis=1), F32))


def _pair_heads(a, axis):
    heads = jnp.split(a, ATT_HEADS, axis=axis)
    return jnp.concatenate([heads[h] for m in range(ATT_GROUP) for h in (m, ATT_GROUP + m)], axis=axis)


def kernel(x, c, ctx, c_ctx, w_ada, b_ada, norm_pre, norm_post, ffn1_w_in, ffn1_w_out, ffn2_w_in, ffn2_w_out,
           mix_w_in, attn_sink, hgrn_lb_fwd, hgrn_lb_bwd, hgrn_norm, w_o_attn, w_o_hgrn, w_out):
    batch, seq, d = x.shape
    ctx_len = ctx.shape[1]
    depth = w_ada.shape[0]
    assert depth == 1, "single-layer problem: the context stream is only read, never written back"
    hw = hgrn_norm.shape[1]
    tm_ffn, tm_in, tm_tail = FFN_PARTS * ROW_HALF, INPROJ_PARTS * ROW_HALF, TAIL_PARTS * ROW_HALF
    bf = lambda a: a.astype(MXU_DTYPE)

    mod_rows = -(-(batch + 1) // 8) * 8
    cs = jnp.concatenate([c, c_ctx[None, :], jnp.zeros((mod_rows - batch - 1, d), F32)], axis=0)
    mod3 = _modulation(cs, w_ada[0], b_ada[0]).reshape(mod_rows, N_MOD, d)
    ctx_row = batch

    xl = x.reshape(batch * seq, d)
    xc = ctx.reshape(batch * ctx_len, d)
    pre = [norm_pre[0, j][None, :] for j in range(3)]
    post = [norm_post[0, j][None, :] for j in range(3)]

    w1_in, w1_out = bf(ffn1_w_in[0]), bf(ffn1_w_out[0])
    xl = _ffn(xl, mod3, lambda i: i // (seq // tm_ffn), 0, pre[0], post[0], w1_in, w1_out, tm_ffn)
    xc = _ffn(xc, mod3, lambda i: ctx_row, 0, pre[0], post[0], w1_in, w1_out, tm_ffn)

    wm = mix_w_in[0]
    kv0 = ATT_WIDTH
    hg0 = kv0 + 2 * ATT_KV_WIDTH
    gate0 = hg0 + 4 * hw
    w_lat = bf(jnp.concatenate([_pair_heads(wm[:, :kv0], 1), wm[:, kv0:gate0]], axis=1))
    w_ctx = bf(jnp.concatenate([wm[:, kv0:hg0], wm[:, hg0 + hw:gate0]], axis=1))
    cos, sin = _rope_tables(seq)
    q, kv, hq, lff, kff, lfb, kfb, hv = _inproj(
        xl, mod3, seq // tm_in, pre[1], cos, sin, hgrn_lb_fwd, hgrn_lb_bwd, w_lat, tm_in)
    kvc, lffc, kffc, lfbc, kfbc, hvc = _inproj_ctx(xc, mod3, ctx_row, pre[1], hgrn_lb_fwd, hgrn_lb_bwd, w_ctx,
                                                   ROW_HALF)

    o_att = _attention(attn_sink[0], q, kv, kvc, batch)
    o_hg = _hgrn(hq, hv, lff, kff, lfb, kfb, hvc, lffc, kffc, lfbc, kfbc, batch)

    woa = bf(_pair_heads(w_o_attn[0], 0))
    xl = _tail(xl, mod3, seq // tm_tail, norm_pre[0], norm_post[0], hgrn_norm, o_att, o_hg,
               bf(wm[:, gate0:]), woa, bf(w_o_hgrn[0]), bf(w_out[0]), bf(ffn2_w_in[0]), bf(ffn2_w_out[0]), tm_tail)
    return xl.reshape(batch, seq, d)
```

```python
import functools

import jax
import jax.numpy as jnp
import numpy as np
from jax import lax
from jax.experimental import pallas as pl
from jax.experimental.pallas import tpu as pltpu

F32 = jnp.float32
MXU_DTYPE = jnp.bfloat16

N_MOD = 9
GRID_W = 64
ATT_HEADS = 8
ATT_KV_HEADS = 2
ATT_GROUP = ATT_HEADS // ATT_KV_HEADS
ATT_HEAD_DIM = 64
ATT_WIDTH = ATT_HEADS * ATT_HEAD_DIM
ATT_KV_WIDTH = ATT_KV_HEADS * ATT_HEAD_DIM
WINDOW = 128
ATT_BLOCK = 128
ATT_SLABS = 1
ATT_QBLOCKS = 16
ATT_LOOKAHEAD = 3
ROPE_BASE = 10000.0
HG_EXPAND = 128
ROW_HALF = 256
FFN_PARTS = 4
INPROJ_PARTS = 4
TAIL_PARTS = 2
EPS = 1e-6
NEG_INF = -1e30
LOG2E = 1.4426950408889634

LANES = 128
HG_CHUNK = 64
HG_SUB = 16
HG_HEADS = 4
HG_UNROLL = 4
HG_UNROLL_EXACT = 2
N_OFF_SEGMENTS = 3
HG_SAFE_LOG2 = 120.0
VMEM_LIMIT = 56 * 1024 * 1024


def _sigmoid(x):
    return 1.0 / (1.0 + jnp.exp(-x))


def _silu(x):
    return x * _sigmoid(x)


def _rms(x, g):
    ms = jnp.mean(x * x, axis=-1, keepdims=True)
    return x * lax.rsqrt(ms + EPS) * g


def _mm(a, b):
    return jnp.dot(a.astype(MXU_DTYPE), b.astype(MXU_DTYPE), preferred_element_type=F32)


def _mm_nt(a, b):
    return lax.dot_general(a.astype(MXU_DTYPE), b.astype(MXU_DTYPE), (((1,), (1,)), ((), ())),
                           preferred_element_type=F32)


def _mm_tn(a, b):
    return lax.dot_general(a.astype(MXU_DTYPE), b.astype(MXU_DTYPE), (((0,), (0,)), ((), ())),
                           preferred_element_type=F32)


def _resident(shape):
    nd = len(shape)
    return pl.BlockSpec(shape, lambda *_: (0,) * nd, pipeline_mode=pl.Buffered(1))


def _params(n_axes):
    return pltpu.CompilerParams(dimension_semantics=("arbitrary",) * n_axes,
                                vmem_limit_bytes=VMEM_LIMIT)


def _mod_kernel(c_ref, w_ref, b_ref, o_ref):
    o_ref[...] = _mm(_silu(c_ref[...]), w_ref[...]) + b_ref[...]


def _modulation(cs, w_ada, b_ada):
    rows, d = cs.shape
    n = w_ada.shape[1]
    tn = n // 8
    return pl.pallas_call(
        _mod_kernel,
        grid=(n // tn,),
        in_specs=[pl.BlockSpec((rows, d), lambda j: (0, 0)),
                  pl.BlockSpec((d, tn), lambda j: (0, j)),
                  pl.BlockSpec((1, tn), lambda j: (0, j))],
        out_specs=pl.BlockSpec((rows, tn), lambda j: (0, j)),
        out_shape=jax.ShapeDtypeStruct((rows, n), F32),
        compiler_params=_params(1),
        name="modulation",
    )(cs, w_ada, b_ada.reshape(1, n))


def _ffn_kernel(x_ref, mod_ref, gpre_ref, gpost_ref, win_ref, wout_ref, o_ref, *, j0, d_ff):
    shift, scale, gate = mod_ref[j0:j0 + 1, :], mod_ref[j0 + 1:j0 + 2, :], mod_ref[j0 + 2:j0 + 3, :]
    halves = [slice(r0, r0 + ROW_HALF) for r0 in range(0, x_ref.shape[0], ROW_HALF)]
    xs = [x_ref[r, :] for r in halves]
    hs = [(_rms(x, gpre_ref[...]) * (1.0 + scale) + shift).astype(MXU_DTYPE) for x in xs]
    gus = [_mm(h, win_ref[...]) for h in hs]
    acts = [(_silu(gu[:, :d_ff]) * gu[:, d_ff:]).astype(MXU_DTYPE) for gu in gus]
    ys = [_mm(act, wout_ref[...]) for act in acts]
    for r, x, y in zip(halves, xs, ys):
        o_ref[r, :] = x + 0.5 * gate * _rms(y, gpost_ref[...])


def _ffn(x, mod3, mod_row, j0, g_pre, g_post, w_in, w_out, tm):
    rows, d = x.shape
    assert rows % tm == 0, (rows, tm)
    d_ff = w_out.shape[0]
    return pl.pallas_call(
        functools.partial(_ffn_kernel, j0=j0, d_ff=d_ff),
        grid=(rows // tm,),
        in_specs=[pl.BlockSpec((tm, d), lambda i: (i, 0)),
                  pl.BlockSpec((None, N_MOD, d), lambda i: (mod_row(i), 0, 0)),
                  _resident((1, d)), _resident((1, d)),
                  _resident(w_in.shape), _resident(w_out.shape)],
        out_specs=pl.BlockSpec((tm, d), lambda i: (i, 0)),
        out_shape=jax.ShapeDtypeStruct((rows, d), F32),
        compiler_params=_params(1),
        name="ffn",
    )(x, mod3, g_pre, g_post, w_in, w_out)


def _lower_bound(lb_ref):
    a = lb_ref[...]
    e = jnp.exp(a - jnp.max(a, axis=0, keepdims=True))
    return e[0:1, :] / jnp.sum(e, axis=0, keepdims=True)


def _forget(z, lb):
    f = lb + (1.0 - lb) * _sigmoid(z)
    return jnp.log2(f), 1.0 - f


def _rope(x, cos, sin):
    w = x.shape[1]
    reps = w // LANES
    cos_w = jnp.concatenate([cos] * reps, axis=1) if reps > 1 else cos
    sin_w = jnp.concatenate([sin] * reps, axis=1) if reps > 1 else sin
    lane = lax.broadcasted_iota(jnp.int32, x.shape, 1)
    quarter = ATT_HEAD_DIM // 4
    first = (lane % (2 * quarter)) < quarter
    partner = jnp.where(first, pltpu.roll(x, w - quarter, 1), pltpu.roll(x, quarter, 1))
    return x * cos_w + partner * sin_w


def _inproj_kernel(x_ref, mod_ref, gpre_ref, cos_ref, sin_ref, lbf_ref, lbb_ref, w_ref,
                   q_ref, kv_ref, hq_ref, lff_ref, kff_ref, lfb_ref, kfb_ref, hv_ref, *, hw):
    shift, scale = mod_ref[3:4, :], mod_ref[4:5, :]
    halves = [slice(r0, r0 + ROW_HALF) for r0 in range(0, x_ref.shape[0], ROW_HALF)]
    hs = [(_rms(x_ref[r, :], gpre_ref[...]) * (1.0 + scale) + shift).astype(MXU_DTYPE) for r in halves]
    c0 = 0

    def proj(width):
        nonlocal c0
        outs = [jnp.dot(h, w_ref[:, c0:c0 + width], preferred_element_type=F32) for h in hs]
        c0 += width
        return zip(halves, outs)

    for r, q in proj(ATT_WIDTH):
        q_ref[r, :] = (_rope(q, cos_ref[r, :], sin_ref[r, :]) * (LOG2E * ATT_HEAD_DIM ** -0.5)).astype(q_ref.dtype)
    for r, kv in proj(2 * ATT_KV_WIDTH):
        kv_ref[r, :ATT_KV_WIDTH] = _rope(kv[:, :ATT_KV_WIDTH], cos_ref[r, :], sin_ref[r, :]).astype(kv_ref.dtype)
        kv_ref[r, ATT_KV_WIDTH:] = kv[:, ATT_KV_WIDTH:].astype(kv_ref.dtype)
    for r, z in proj(hw):
        hq_ref[r, :] = _silu(z).astype(hq_ref.dtype)
    for lb_ref, lf_ref, kf_ref in ((lbf_ref, lff_ref, kff_ref), (lbb_ref, lfb_ref, kfb_ref)):
        for r, z in proj(hw):
            lf, kf = _forget(z, _lower_bound(lb_ref))
            lf_ref[r, :] = lf
            kf_ref[r, :] = kf.astype(kf_ref.dtype)
    for r, z in proj(hw):
        hv_ref[r, :] = z.astype(hv_ref.dtype)


def _inproj(x, mod3, tiles_per_batch, g_pre, cos, sin, lb_f, lb_b, w, tm):
    rows, d = x.shape
    assert rows % tm == 0, (rows, tm)
    hw = lb_f.shape[1]
    row = lambda width: pl.BlockSpec((tm, width), lambda i: (i, 0))
    widths = [ATT_WIDTH, 2 * ATT_KV_WIDTH, hw, hw, hw, hw, hw, hw]
    dtypes = [MXU_DTYPE, MXU_DTYPE, MXU_DTYPE, F32, MXU_DTYPE, F32, MXU_DTYPE, MXU_DTYPE]
    return pl.pallas_call(
        functools.partial(_inproj_kernel, hw=hw),
        grid=(rows // tm,),
        in_specs=[row(d),
                  pl.BlockSpec((None, N_MOD, d), lambda i: (i // tiles_per_batch, 0, 0)),
                  _resident((1, d)),
                  pl.BlockSpec((tm, LANES), lambda i: (i % tiles_per_batch, 0)),
                  pl.BlockSpec((tm, LANES), lambda i: (i % tiles_per_batch, 0)),
                  _resident(lb_f.shape), _resident(lb_b.shape), _resident(w.shape)],
        out_specs=[row(wd) for wd in widths],
        out_shape=[jax.ShapeDtypeStruct((rows, wd), dt) for wd, dt in zip(widths, dtypes)],
        compiler_params=_params(1),
        name="inproj",
    )(x, mod3, g_pre, cos, sin, lb_f, lb_b, w)


def _inproj_ctx_kernel(x_ref, mod_ref, gpre_ref, lbf_ref, lbb_ref, w_ref,
                       kv_ref, lff_ref, kff_ref, lfb_ref, kfb_ref, hv_ref, *, hw):
    x = x_ref[...]
    shift, scale = mod_ref[3:4, :], mod_ref[4:5, :]
    h = (_rms(x, gpre_ref[...]) * (1.0 + scale) + shift).astype(MXU_DTYPE)
    kvw = 2 * ATT_KV_WIDTH
    kv_ref[...] = jnp.dot(h, w_ref[:, :kvw], preferred_element_type=F32).astype(kv_ref.dtype)
    lf, kf = _forget(jnp.dot(h, w_ref[:, kvw:kvw + hw], preferred_element_type=F32), _lower_bound(lbf_ref))
    lff_ref[...] = lf
    kff_ref[...] = kf.astype(kff_ref.dtype)
    lf, kf = _forget(jnp.dot(h, w_ref[:, kvw + hw:kvw + 2 * hw], preferred_element_type=F32),
                     _lower_bound(lbb_ref))
    lfb_ref[...] = lf
    kfb_ref[...] = kf.astype(kfb_ref.dtype)
    hv_ref[...] = jnp.dot(h, w_ref[:, kvw + 2 * hw:], preferred_element_type=F32).astype(hv_ref.dtype)


def _inproj_ctx(xc, mod3, ctx_row, g_pre, lb_f, lb_b, w, tm):
    rows, d = xc.shape
    assert rows % tm == 0, (rows, tm)
    hw = lb_f.shape[1]
    row = lambda width: pl.BlockSpec((tm, width), lambda i: (i, 0))
    widths = [2 * ATT_KV_WIDTH, hw, hw, hw, hw, hw]
    dtypes = [MXU_DTYPE, F32, MXU_DTYPE, F32, MXU_DTYPE, MXU_DTYPE]
    return pl.pallas_call(
        functools.partial(_inproj_ctx_kernel, hw=hw),
        grid=(rows // tm,),
        in_specs=[row(d),
                  pl.BlockSpec((None, N_MOD, d), lambda i: (ctx_row, 0, 0)),
                  _resident((1, d)), _resident(lb_f.shape), _resident(lb_b.shape), _resident(w.shape)],
        out_specs=[row(wd) for wd in widths],
        out_shape=[jax.ShapeDtypeStruct((rows, wd), dt) for wd, dt in zip(widths, dtypes)],
        compiler_params=_params(1),
        name="inproj_ctx",
    )(xc, mod3, g_pre, lb_f, lb_b, w)


def _attn_kernel(sink_ref, q_ref, kv_ref, kvc_ref, o_ref, *, seq, band):
    step = pl.program_id(1)
    kvc = kvc_ref[...]
    low_lane = lax.broadcasted_iota(jnp.int32, (ATT_BLOCK, LANES), 1) < ATT_HEAD_DIM
    low_row = lax.broadcasted_iota(jnp.int32, (LANES, ATT_BLOCK), 0) < ATT_HEAD_DIM
    heads_per_group = 2 * ATT_SLABS

    def block_keys(qb):
        n = step * ATT_QBLOCKS + qb
        start = pl.multiple_of(jnp.clip(n * ATT_BLOCK - ATT_BLOCK, 0, seq - band), ATT_BLOCK)
        kvb = kv_ref[pl.ds(start, band), :]
        k_all = jnp.concatenate([kvb[:, :ATT_KV_WIDTH], kvc[:, :ATT_KV_WIDTH]], axis=0)
        v_t = jnp.concatenate([kvb[:, ATT_KV_WIDTH:], kvc[:, ATT_KV_WIDTH:]], axis=0).T
        kpos = start + lax.broadcasted_iota(jnp.int32, (band, ATT_BLOCK), 0)
        qpos = n * ATT_BLOCK + lax.broadcasted_iota(jnp.int32, (band, ATT_BLOCK), 1)
        valid = jnp.abs(kpos - qpos) <= WINDOW
        return k_all, v_t, jnp.concatenate([valid] * heads_per_group, axis=1)

    keys = [block_keys(qb) for qb in range(ATT_QBLOCKS)]

    def unit_scores(qb, m0):
        q_rows, sink_row = [], []
        for m in range(m0, m0 + ATT_SLABS):
            qm = q_ref[qb * ATT_BLOCK:(qb + 1) * ATT_BLOCK, m * LANES:(m + 1) * LANES]
            zero = jnp.zeros_like(qm)
            q_rows += [jnp.where(low_lane, qm, zero), jnp.where(low_lane, zero, qm)]
            sink_row += [jnp.full((1, ATT_BLOCK), sink_ref[m] * LOG2E, F32),
                         jnp.full((1, ATT_BLOCK), sink_ref[ATT_GROUP + m] * LOG2E, F32)]
        return _mm_nt(keys[qb][0], jnp.concatenate(q_rows, axis=0)), jnp.concatenate(sink_row, axis=1)

    units = [(qb, m0) for qb in range(ATT_QBLOCKS) for m0 in range(0, ATT_GROUP, ATT_SLABS)]

    def finish(o_t, den, qb, m0):
        o_t = o_t * (1.0 / den)
        for j, m in enumerate(range(m0, m0 + ATT_SLABS)):
            o_lo = o_t[:, (2 * j) * ATT_BLOCK:(2 * j + 1) * ATT_BLOCK]
            o_hi = o_t[:, (2 * j + 1) * ATT_BLOCK:(2 * j + 2) * ATT_BLOCK]
            o_ref[qb * ATT_BLOCK:(qb + 1) * ATT_BLOCK, m * LANES:(m + 1) * LANES] = (
                jnp.where(low_row, o_lo, o_hi).T.astype(o_ref.dtype))

    pending = [unit_scores(*unit) for unit in units[:ATT_LOOKAHEAD]]
    unfinished = None
    for u, (qb, m0) in enumerate(units):
        if u + ATT_LOOKAHEAD < len(units):
            pending.append(unit_scores(*units[u + ATT_LOOKAHEAD]))
        s_t, sink2 = pending[u]
        _, v_t, valid = keys[qb]
        s_t = jnp.concatenate([jnp.where(valid, s_t[:band], NEG_INF), s_t[band:]], axis=0)
        m = jnp.maximum(jnp.max(s_t, axis=0, keepdims=True), sink2)
        e = jnp.exp2(s_t - m)
        den = jnp.sum(e, axis=0, keepdims=True) + jnp.exp2(sink2 - m)
        o_t = _mm(v_t, e)
        if unfinished is not None:
            finish(*unfinished)
        unfinished = (o_t, den, qb, m0)
    finish(*unfinished)


def _attention(sink, q, kv, kvc, batch):
    rows, _ = q.shape
    seq = rows // batch
    ctx_len = kvc.shape[0] // batch
    tq = ATT_QBLOCKS * ATT_BLOCK
    steps = seq // tq
    band = 3 * ATT_BLOCK
    return pl.pallas_call(
        functools.partial(_attn_kernel, seq=seq, band=band),
        grid=(batch, steps),
        in_specs=[pl.BlockSpec(memory_space=pltpu.SMEM),
                  pl.BlockSpec((tq, ATT_WIDTH), lambda b, n: (b * steps + n, 0)),
                  pl.BlockSpec((seq, 2 * ATT_KV_WIDTH), lambda b, n: (b, 0)),
                  pl.BlockSpec((ctx_len, 2 * ATT_KV_WIDTH), lambda b, n: (b, 0))],
        out_specs=pl.BlockSpec((tq, ATT_WIDTH), lambda b, n: (b * steps + n, 0)),
        out_shape=jax.ShapeDtypeStruct((rows, ATT_WIDTH), MXU_DTYPE),
        compiler_params=_params(2),
        name="attention",
    )(sink, q, kv, kvc)


def _cum_decays(lf_fwd, lf_bwd, tri):
    c = lf_fwd.shape[0]
    lf = jnp.concatenate([lf_fwd, lf_bwd], axis=1)
    hi = lf.astype(jnp.bfloat16)
    r1 = lf - hi.astype(F32)
    mid = r1.astype(jnp.bfloat16)
    lo = (r1 - mid.astype(F32)).astype(jnp.bfloat16)
    both = jnp.dot(tri, jnp.concatenate([hi, mid, lo], axis=0), preferred_element_type=F32)
    return [both[:c, :LANES], both[c:, LANES:]]


def _hg_state_update(st, k, v, b, b_end):
    k_dec = k * jnp.exp2(b_end - b)
    return st * jnp.exp2(b_end) + _mm_tn(v, k_dec)


def _hg_diag_tiles(q4, k4, b4, rev):
    nblk, sub, _ = q4.shape
    half = sub // 2
    tiles = []
    for s in range(sub):
        ks, bs = k4[:, s:s + 1, :], b4[:, s:s + 1, :]
        halves = []
        for hf in range(2):
            needed = (hf == 0 or s >= half) if rev else (hf == 1 or s < half)
            if needed:
                rows = slice(hf * half, (hf + 1) * half)
                halves.append(q4[:, rows] * ks * jnp.exp2(jnp.minimum(b4[:, rows] - bs, 0.0)))
            else:
                halves.append(jnp.zeros((nblk, half, LANES), F32))
        tiles.append(jnp.concatenate(halves, axis=1).reshape(nblk * sub, LANES).astype(MXU_DTYPE))
    return jnp.concatenate(tiles, axis=1)


def _hg_local(q, k, b, rev, exact):
    c, sub = HG_CHUNK, HG_SUB
    nblk = c // sub
    q4, k4, b4 = (a.reshape(nblk, sub, LANES) for a in (q, k, b))
    zero = jnp.zeros((sub, LANES), F32)

    def factors(q_blocks, k_blocks, anchor):
        ba = b[anchor:anchor + 1]
        qs = [q4[i] * jnp.exp2(b4[i] - ba) if i in q_blocks else zero for i in range(nblk)]
        ks = [k4[i] * jnp.exp2(ba - b4[i]) if i in k_blocks else zero for i in range(nblk)]
        return jnp.concatenate(qs, axis=0), jnp.concatenate(ks, axis=0)

    if rev:
        pairs = [factors((0,), (1,), sub), factors((2,), (3,), 3 * sub), factors((0, 1), (2, 3), 2 * sub)]
    else:
        pairs = [factors((1,), (0,), sub - 1), factors((3,), (2,), 3 * sub - 1), factors((2, 3), (0, 1), 2 * sub - 1)]
    if not exact:
        anchors = [b4[i, (sub - 1 if rev else 0):(sub if rev else 1), :] for i in range(nblk)]
        pairs.append((jnp.concatenate([q4[i] * jnp.exp2(b4[i] - anchors[i]) for i in range(nblk)], axis=0),
                      jnp.concatenate([k4[i] * jnp.exp2(anchors[i] - b4[i]) for i in range(nblk)], axis=0)))
    qcat = jnp.concatenate([p[0] for p in pairs], axis=1).astype(MXU_DTYPE)
    kcat = jnp.concatenate([p[1] for p in pairs], axis=1)
    kcat_t = jnp.concatenate([kcat, jnp.zeros((LANES - c, kcat.shape[1]), F32)], axis=0).T.astype(MXU_DTYPE)
    return (_hg_diag_tiles(q4, k4, b4, rev) if exact else None), qcat, kcat_t


def _hgrn_kernel(q_ref, v_ref, lff_ref, kff_ref, lfb_ref, kfb_ref,
                 vc_ref, lffc_ref, kffc_ref, lfbc_ref, kfbc_ref, sel_ref, o_ref,
                 s_ref, z_ref, qc_ref, kc_ref, b_ref):
    c, sub = HG_CHUNK, HG_SUB
    seq, ctx_len = q_ref.shape[0], vc_ref.shape[0]
    n, nc = seq // c, ctx_len // c
    r = lax.broadcasted_iota(jnp.int32, (c, c), 0)
    s = lax.broadcasted_iota(jnp.int32, (c, c), 1)
    tri = jnp.concatenate([jnp.concatenate([(s <= r).astype(jnp.bfloat16)] * 3, axis=1),
                           jnp.concatenate([(s >= r).astype(jnp.bfloat16)] * 3, axis=1)], axis=0)
    row = lax.broadcasted_iota(jnp.int32, (c, LANES), 0)
    lane = lax.broadcasted_iota(jnp.int32, (c, LANES), 1)
    own = (lane // sub) == (row // sub)
    dmask_f = own & ((lane % sub) <= (row % sub))
    dmask_b = own & ((lane % sub) >= (row % sub))
    pad = jnp.zeros((LANES - c, LANES), MXU_DTYPE)

    heads = [slice(hd * LANES, (hd + 1) * LANES) for hd in range(q_ref.shape[1] // LANES)]
    dirs = ((kff_ref, dmask_f, False), (kfb_ref, dmask_b, True))

    for hd, ln in enumerate(heads):
        sf, sb = jnp.zeros(s_ref.shape[2:], F32), jnp.zeros(s_ref.shape[2:], F32)
        for i in range(nc):
            rf, rb = slice(i * c, (i + 1) * c), slice((nc - 1 - i) * c, (nc - i) * c)
            bf, bb = _cum_decays(lffc_ref[rf, ln], lfbc_ref[rb, ln], tri)
            sf = _hg_state_update(sf, kffc_ref[rf, ln].astype(F32), vc_ref[rf, ln], bf, bf[c - 1:c, :])
            sb = _hg_state_update(sb, kfbc_ref[rb, ln].astype(F32), vc_ref[rb, ln], bb, bb[0:1, :])
        s_ref[hd, 0] = sf
        s_ref[hd, 1] = sb

    def chunk_rows(i, rev):
        return pl.ds(pl.multiple_of(((n - 1 - i) if rev else i) * c, c), c)

    def cum_decays(i):
        return [_cum_decays(lff_ref[chunk_rows(i, False), ln], lfb_ref[chunk_rows(i, True), ln], tri)
                for ln in heads]

    def stage_local(i, bs, slot, exact):
        for hd, ln in enumerate(heads):
            for j, (k_ref, _, rev) in enumerate(dirs):
                rows = chunk_rows(i, rev)
                zcat, qcat, kcat = _hg_local(q_ref[rows, ln].astype(F32), k_ref[rows, ln].astype(F32),
                                             bs[hd][j], rev, exact)
                if exact:
                    z_ref[slot, hd, j * c:(j + 1) * c, :] = zcat
                qc_ref[slot, hd, j, :, 0:qcat.shape[1]] = qcat
                kc_ref[slot, hd, j, 0:kcat.shape[0], :] = kcat
                b_ref[slot, hd, j] = bs[hd][j]

    def step(i, slot, accumulate, exact):
        off = N_OFF_SEGMENTS * LANES
        nxt = jnp.minimum(i + 1, n - 1)
        b_next = cum_decays(nxt)
        started = []
        for hd, ln in enumerate(heads):
            if exact:
                diag_both = jnp.dot(z_ref[slot, hd], sel_ref[...], preferred_element_type=F32)
            for j, (k_ref, dmask, rev) in enumerate(dirs):
                rows = chunk_rows(i, rev)
                q, k = q_ref[rows, ln].astype(F32), k_ref[rows, ln].astype(F32)
                v, b = v_ref[rows, ln], b_ref[slot, hd, j]
                st = s_ref[hd, j]
                b_end = b[0:1, :] if rev else b[c - 1:c, :]
                below = jnp.dot(qc_ref[slot, hd, j, :, 0:off], kc_ref[slot, hd, j, 0:off, :],
                                preferred_element_type=F32)
                if exact:
                    diag = diag_both[j * c:(j + 1) * c]
                else:
                    diag = jnp.dot(qc_ref[slot, hd, j, :, off:off + LANES], kc_ref[slot, hd, j, off:off + LANES, :],
                                   preferred_element_type=F32)
                rhs = jnp.concatenate([st.T.astype(MXU_DTYPE), v.astype(MXU_DTYPE), pad], axis=0)
                started.append((hd, ln, j, rows, dmask, (q * jnp.exp2(b)).astype(MXU_DTYPE), rhs, below, diag,
                                _hg_state_update(st, k, v, b, b_end)))
        stage_local(nxt, b_next, 1 - slot, exact)
        for hd, ln, j, rows, dmask, q_dec, rhs, below, diag, st_new in started:
            scores = jnp.where(dmask, diag, 0.0) + below
            o = jnp.dot(jnp.concatenate([q_dec, scores.astype(MXU_DTYPE)], axis=1), rhs,
                        preferred_element_type=F32)
            s_ref[hd, j] = st_new
            if accumulate:
                o_ref[rows, ln] += o
            else:
                o_ref[rows, ln] = o

    def scan(exact):
        stage_local(0, cum_decays(0), 0, exact)

        unroll = HG_UNROLL_EXACT if exact else HG_UNROLL

        def steps(ii, carry, accumulate):
            for u in range(unroll):
                step(ii * unroll + u, u % 2, accumulate, exact)
            return carry

        half_trips = n // (2 * unroll)
        lax.fori_loop(0, half_trips, functools.partial(steps, accumulate=False), 0)
        lax.fori_loop(half_trips, 2 * half_trips, functools.partial(steps, accumulate=True), 0)

    steepest = jnp.minimum(jnp.min(lff_ref[...]), jnp.min(lfb_ref[...]))
    lax.cond(steepest * (sub - 1) >= -HG_SAFE_LOG2, lambda: scan(False), lambda: scan(True))


def _hgrn(hq, hv, lff, kff, lfb, kfb, hvc, lffc, kffc, lfbc, kfbc, batch):
    rows, hw = hq.shape
    seq = rows // batch
    ctx_len = hvc.shape[0] // batch
    heads = hw // HG_EXPAND
    assert seq % (2 * HG_UNROLL * HG_CHUNK) == 0 and ctx_len % HG_CHUNK == 0
    assert HG_UNROLL % HG_UNROLL_EXACT == 0 and HG_UNROLL_EXACT % 2 == 0
    n_seg = N_OFF_SEGMENTS + 1
    src = np.arange(HG_SUB * LANES, dtype=np.int32)[:, None] // LANES
    sel = jnp.asarray(src == np.arange(LANES, dtype=np.int32)[None, :] % HG_SUB, MXU_DTYPE)
    assert heads % HG_HEADS == 0
    width = HG_HEADS * HG_EXPAND
    lat = pl.BlockSpec((seq, width), lambda b, h: (b, h))
    ctx = pl.BlockSpec((ctx_len, width), lambda b, h: (b, h))
    return pl.pallas_call(
        _hgrn_kernel,
        grid=(batch, heads // HG_HEADS),
        in_specs=[lat] * 6 + [ctx] * 5 + [_resident(sel.shape)],
        out_specs=lat,
        out_shape=jax.ShapeDtypeStruct((rows, hw), F32),
        scratch_shapes=[
            pltpu.VMEM((HG_HEADS, 2, HG_EXPAND, HG_EXPAND), F32),
            pltpu.VMEM((2, HG_HEADS, 2 * HG_CHUNK, HG_SUB * LANES), MXU_DTYPE),
            pltpu.VMEM((2, HG_HEADS, 2, HG_CHUNK, n_seg * LANES), MXU_DTYPE),
            pltpu.VMEM((2, HG_HEADS, 2, n_seg * LANES, LANES), MXU_DTYPE),
            pltpu.VMEM((2, HG_HEADS, 2, HG_CHUNK, LANES), F32)],
        compiler_params=_params(2),
        name="hgrn",
    )(hq, hv, lff, kff, lfb, kfb, hvc, lffc, kffc, lfbc, kfbc, sel)


def _tail_kernel(x_ref, mod_ref, pre_ref, post_ref, ghn_ref, oa_ref, oh_ref,
                 wg_ref, woa_ref, woh_ref, wout_ref, win_ref, wo2_ref, o_ref, *, hw, d_ff):
    d = x_ref.shape[1]
    mod = lambda j: mod_ref[j:j + 1, :]
    bf = lambda a: a.astype(MXU_DTYPE)
    halves = [slice(r0, r0 + ROW_HALF) for r0 in range(0, x_ref.shape[0], ROW_HALF)]
    xs = [x_ref[r, :] for r in halves]
    gs = [_mm(bf(_rms(x, pre_ref[1:2, :]) * (1.0 + mod(4)) + mod(3)), wg_ref[...]) for x in xs]
    rs = [bf(_rms(oh_ref[r, :], ghn_ref[...]) * _silu(g[:, :hw])) for r, g in zip(halves, gs)]
    zs = [bf(_sigmoid(g[:, hw:hw + d]) * _mm(oa_ref[r, :], woa_ref[...])
             + _sigmoid(g[:, hw + d:]) * _mm(rr, woh_ref[...])) for r, g, rr in zip(halves, gs, rs)]
    xs = [x + mod(5) * _rms(_mm(z, wout_ref[...]), post_ref[1:2, :]) for x, z in zip(xs, zs)]
    hs = [bf(_rms(x, pre_ref[2:3, :]) * (1.0 + mod(7)) + mod(6)) for x in xs]
    gus = [_mm(h, win_ref[...]) for h in hs]
    acts = [bf(_silu(gu[:, :d_ff]) * gu[:, d_ff:]) for gu in gus]
    ys = [_mm(act, wo2_ref[...]) for act in acts]
    for r, x, y in zip(halves, xs, ys):
        o_ref[r, :] = x + 0.5 * mod(8) * _rms(y, post_ref[2:3, :])


def _tail(x, mod3, tiles_per_batch, pre, post, g_hnorm, oa, oh, wg, woa, woh, wout, w_in, w_out, tm):
    rows, d = x.shape
    assert rows % tm == 0, (rows, tm)
    row = lambda width: pl.BlockSpec((tm, width), lambda i: (i, 0))
    return pl.pallas_call(
        functools.partial(_tail_kernel, hw=g_hnorm.shape[1], d_ff=w_out.shape[0]),
        grid=(rows // tm,),
        in_specs=[row(d),
                  pl.BlockSpec((None, N_MOD, d), lambda i: (i // tiles_per_batch, 0, 0)),
                  _resident(pre.shape), _resident(post.shape), _resident(g_hnorm.shape),
                  row(oa.shape[1]), row(oh.shape[1])]
                 + [_resident(w.shape) for w in (wg, woa, woh, wout, w_in, w_out)],
        out_specs=row(d),
        out_shape=jax.ShapeDtypeStruct((rows, d), F32),
        compiler_params=_params(1),
        name="merge_ffn",
    )(x, mod3, pre, post, g_hnorm, oa, oh, wg, woa, woh, wout, w_in, w_out)


def _rope_tables(seq):
    quarter = ATT_HEAD_DIM // 4
    t = np.arange(seq, dtype=np.int32)
    freqs = np.float32(ROPE_BASE) ** (-np.arange(quarter, dtype=np.float32) / np.float32(quarter))
    cos, sin = [], []
    for pos in (t // GRID_W, t % GRID_W):
        ang = pos.astype(np.float32)[:, None] * freqs[None, :]
        cos += [np.cos(ang), np.cos(ang)]
        sin += [-np.sin(ang), np.sin(ang)]
    reps = LANES // ATT_HEAD_DIM
    return (jnp.asarray(np.concatenate(cos * reps, axis=1), F32),
            jnp.asarray(np.concatenate(sin * reps, axis=1), F32))


def _pair_heads(a, axis):
    heads = jnp.split(a, ATT_HEADS, axis=axis)
    return jnp.concatenate([heads[h] for m in range(ATT_GROUP) for h in (m, ATT_GROUP + m)], axis=axis)


def kernel(x, c, ctx, c_ctx, w_ada, b_ada, norm_pre, norm_post, ffn1_w_in, ffn1_w_out, ffn2_w_in, ffn2_w_out,
           mix_w_in, attn_sink, hgrn_lb_fwd, hgrn_lb_bwd, hgrn_norm, w_o_attn, w_o_hgrn, w_out):
    batch, seq, d = x.shape
    ctx_len = ctx.shape[1]
    depth = w_ada.shape[0]
    assert depth == 1, "single-layer problem: the context stream is only read, never written back"
    hw = hgrn_norm.shape[1]
    tm_ffn, tm_in, tm_tail = FFN_PARTS * ROW_HALF, INPROJ_PARTS * ROW_HALF, TAIL_PARTS * ROW_HALF
    bf = lambda a: a.astype(MXU_DTYPE)

    mod_rows = -(-(batch + 1) // 8) * 8
    cs = jnp.concatenate([c, c_ctx[None, :], jnp.zeros((mod_rows - batch - 1, d), F32)], axis=0)
    mod3 = _modulation(cs, w_ada[0], b_ada[0]).reshape(mod_rows, N_MOD, d)
    ctx_row = batch

    xl = x.reshape(batch * seq, d)
    xc = ctx.reshape(batch * ctx_len, d)
    pre = [norm_pre[0, j][None, :] for j in range(3)]
    post = [norm_post[0, j][None, :] for j in range(3)]

    w1_in, w1_out = bf(ffn1_w_in[0]), bf(ffn1_w_out[0])
    xl = _ffn(xl, mod3, lambda i: i // (seq // tm_ffn), 0, pre[0], post[0], w1_in, w1_out, tm_ffn)
    xc = _ffn(xc, mod3, lambda i: ctx_row, 0, pre[0], post[0], w1_in, w1_out, tm_ffn)

    wm = mix_w_in[0]
    kv0 = ATT_WIDTH
    hg0 = kv0 + 2 * ATT_KV_WIDTH
    gate0 = hg0 + 4 * hw
    w_lat = bf(jnp.concatenate([_pair_heads(wm[:, :kv0], 1), wm[:, kv0:gate0]], axis=1))
    w_ctx = bf(jnp.concatenate([wm[:, kv0:hg0], wm[:, hg0 + hw:gate0]], axis=1))
    cos, sin = _rope_tables(seq)
    q, kv, hq, lff, kff, lfb, kfb, hv = _inproj(
        xl, mod3, seq // tm_in, pre[1], cos, sin, hgrn_lb_fwd, hgrn_lb_bwd, w_lat, tm_in)
    kvc, lffc, kffc, lfbc, kfbc, hvc = _inproj_ctx(xc, mod3, ctx_row, pre[1], hgrn_lb_fwd, hgrn_lb_bwd, w_ctx,
                                                   ROW_HALF)

    o_att = _attention(attn_sink[0], q, kv, kvc, batch)
    o_hg = _hgrn(hq, hv, lff, kff, lfb, kfb, hvc, lffc, kffc, lfbc, kfbc, batch)

    woa = bf(_pair_heads(w_o_attn[0], 0))
    xl = _tail(xl, mod3, seq // tm_tail, norm_pre[0], norm_post[0], hgrn_norm, o_att, o_hg,
               bf(wm[:, gate0:]), woa, bf(w_o_hgrn[0]), bf(w_out[0]), bf(ffn2_w_in[0]), bf(ffn2_w_out[0]), tm_tail)
    return xl.reshape(batch, seq, d)
```
